```python
import jax, jax.numpy as jnp
from jax import lax
import numpy as np

D_MODEL = 1024
BATCH = 2
SEQ = 16384
DEPTH = 1
DEC_BATCH = 128
DEC_SEQ = 1
PAST_LEN = 8192
PAGE_SIZE = 128

D_CONV = D_MODEL
CONV_WIDTH = 31
HEAD_DIM = 64
HEADS_PER_GROUP = 8
DIL_GROUPS = ((128, 1), (512, 4), (2048, 16))
N_DIL_GROUPS = 3
D_ATT = N_DIL_GROUPS * HEADS_PER_GROUP * HEAD_DIM
D_SLOT = HEADS_PER_GROUP * HEAD_DIM
Q_BLOCK = 128
ROPE_THETA = 10000.0
N_EXPERTS = 256
TOP_K = 8
N_EXPERT_GROUPS = 8
TOPK_GROUPS = 4
D_EXPERT = 256
ROUTED_SCALE = 2.5
MOE_BLOCK = 128
D_IN = 2 * D_CONV + 3 * D_ATT + 2 * D_MODEL
RMS_EPS = 1e-6
LN_EPS = 1e-5

kernel_name = 'gated_conformer_dilated_attn_moe_step'


def rms_norm(x, g):
    xf = x.astype(jnp.float32)
    y = xf * lax.rsqrt(jnp.mean(xf * xf, axis=-1, keepdims=True) + RMS_EPS)
    return (y * g.astype(jnp.float32)).astype(x.dtype)


def layer_norm(x, g, b):
    xf = x.astype(jnp.float32)
    mu = jnp.mean(xf, axis=-1, keepdims=True)
    var = jnp.mean(jnp.square(xf - mu), axis=-1, keepdims=True)
    y = (xf - mu) * lax.rsqrt(var + LN_EPS)
    return (y * g.astype(jnp.float32) + b.astype(jnp.float32)).astype(x.dtype)


def rope(x, pos):
    half = HEAD_DIM // 2
    inv_freq = ROPE_THETA ** (-jnp.arange(half, dtype=jnp.float32) / half)
    ang = pos.astype(jnp.float32)[:, None] * inv_freq[None, :]
    ang = ang.reshape(ang.shape[0], *([1] * (x.ndim - 3)), half)
    cos, sin = jnp.cos(ang), jnp.sin(ang)
    xf = x.astype(jnp.float32)
    x1, x2 = xf[..., :half], xf[..., half:]
    return jnp.concatenate([x1 * cos - x2 * sin, x2 * cos + x1 * sin], axis=-1).astype(x.dtype)


def ada_modulate(x, c, g, w_ada, b_ada):
    mod = (jax.nn.silu(c) @ w_ada + b_ada)[:, None, :]
    shift, scale, gate = jnp.split(mod, 3, axis=-1)
    return rms_norm(x, g) * (1.0 + scale) + shift, gate


def mixer_inputs(h, pos, w_in):
    B, S, _ = h.shape
    cuts, acc = [], 0
    for size in (D_CONV, D_CONV, D_ATT, D_ATT, D_ATT, D_MODEL):
        acc += size
        cuts.append(acc)
    a, b, q, k, v, g_conv, g_att = jnp.split(h @ w_in, cuts, axis=-1)
    u = a * jax.nn.sigmoid(b)
    heads = lambda t: t.reshape(B, S, N_DIL_GROUPS, HEADS_PER_GROUP, HEAD_DIM)
    return u, rope(heads(q), pos), rope(heads(k), pos), heads(v), g_conv, g_att


def causal_depthwise(u_ext, w_dw, b_dw):
    y = lax.conv_general_dilated(u_ext, w_dw[:, None, :].astype(u_ext.dtype), window_strides=(1,),
                                 padding='VALID', dimension_numbers=('NWC', 'WIO', 'NWC'),
                                 feature_group_count=D_CONV)
    return y + b_dw


def dilated_band_attention(q, k, v, dil, n_steps):
    B, S, H, Dh = q.shape
    span = dil * Q_BLOCK
    s_pad = -(-S // span) * span
    L = s_pad // dil
    nb = L // Q_BLOCK

    def to_streams(t):
        t = jnp.pad(t, ((0, 0), (0, s_pad - S), (0, 0), (0, 0)))
        t = t.reshape(B, L, dil, H, Dh).transpose(0, 2, 1, 3, 4)
        return t.reshape(B * dil, nb, Q_BLOCK, H, Dh)

    def with_prev(t):
        prev = jnp.pad(t[:, :-1], ((0, 0), (1, 0), (0, 0), (0, 0), (0, 0)))
        return jnp.concatenate([prev, t], axis=2)

    qb = to_streams(q)
    kk, vv = with_prev(to_streams(k)), with_prev(to_streams(v))
    s = jnp.einsum('nbqhd,nbkhd->nbhqk', qb, kk, preferred_element_type=jnp.float32) * (HEAD_DIM ** -0.5)
    qi = jnp.arange(Q_BLOCK)[:, None]
    ki = jnp.arange(2 * Q_BLOCK)[None, :]
    dist = Q_BLOCK + qi - ki
    in_band = (dist >= 0) & (dist <= n_steps)
    has_prev = (jnp.arange(nb) > 0)[:, None, None] | (ki >= Q_BLOCK)[None]
    mask = in_band[None] & has_prev
    s = jnp.where(mask[None, :, None], s, -jnp.inf)
    m = jnp.max(s, axis=-1, keepdims=True)
    p = jnp.exp(s - m)
    den = jnp.sum(p, axis=-1, keepdims=True)
    o = jnp.einsum('nbhqk,nbkhd->nbqhd', p / den, vv.astype(jnp.float32))
    lse = (m + jnp.log(den))[..., 0]
    o = o.reshape(B, dil, L, H, Dh).transpose(0, 2, 1, 3, 4).reshape(B, s_pad, H, Dh)[:, :S]
    lse = lse.transpose(0, 1, 3, 2).reshape(B, dil, L, H).transpose(0, 2, 1, 3).reshape(B, s_pad, H)[:, :S]
    return o, lse


def dilated_window_decode(q, k, v, kv_cache, dil, n_steps):
    T = q.shape[1]
    n_buf = kv_cache.shape[1]
    kv_all = jnp.concatenate([kv_cache.astype(k.dtype), jnp.stack([k, v], axis=2)], axis=1)
    idx = n_buf + jnp.arange(T)[:, None] - dil * jnp.arange(n_steps + 1)[None, :]
    valid = idx >= 0
    kv_g = kv_all[:, jnp.maximum(idx, 0)]
    s = jnp.einsum('bthd,btjhd->bthj', q, kv_g[:, :, :, 0], preferred_element_type=jnp.float32) * (HEAD_DIM ** -0.5)
    s = jnp.where(valid[None, :, None, :], s, -jnp.inf)
    m = jnp.max(s, axis=-1, keepdims=True)
    p = jnp.exp(s - m)
    den = jnp.sum(p, axis=-1, keepdims=True)
    o = jnp.einsum('bthj,btjhd->bthd', p / den, kv_g[:, :, :, 1].astype(jnp.float32))
    lse = (m + jnp.log(den))[..., 0]
    return o, lse, kv_all[:, T:]


def mixer_output(y_dw, outs, lses, g_conv, g_att, ln_g, ln_b, w_conv_out, w_att_out, w_o):
    conv_out = jax.nn.silu(layer_norm(y_dw, ln_g, ln_b)) @ w_conv_out
    wts = jax.nn.softmax(jnp.stack(lses, axis=0), axis=0)
    att = sum(wts[gi][..., None] * o for gi, o in enumerate(outs))
    B, S = att.shape[:2]
    att_out = att.reshape(B, S, D_SLOT).astype(y_dw.dtype) @ w_att_out
    merged = jax.nn.sigmoid(g_conv) * conv_out + jax.nn.sigmoid(g_att) * att_out
    return merged @ w_o


def mixer_prompt(h, pos, w_in, w_dw, b_dw, ln_g, ln_b, w_conv_out, w_att_out, w_o):
    S = h.shape[1]
    u, q, k, v, g_conv, g_att = mixer_inputs(h, pos, w_in)
    u_ext = jnp.pad(u, ((0, 0), (CONV_WIDTH - 1, 0), (0, 0)))
    y_dw = causal_depthwise(u_ext, w_dw, b_dw)
    conv_state = u_ext[:, u_ext.shape[1] - (CONV_WIDTH - 1):]
    outs, lses, kv_rows = [], [], []
    for gi, (win, dil) in enumerate(DIL_GROUPS):
        o, lse = dilated_band_attention(q[:, :, gi], k[:, :, gi], v[:, :, gi], dil, win // dil)
        outs.append(o)
        lses.append(lse)
        keep = min(win, S)
        kv_rows.append(jnp.stack([k[:, S - keep:, gi], v[:, S - keep:, gi]], axis=2))
    y = mixer_output(y_dw, outs, lses, g_conv, g_att, ln_g, ln_b, w_conv_out, w_att_out, w_o)
    return y, kv_rows, conv_state


def mixer_sample(h, pos, kv_caches, conv_cache, w_in, w_dw, b_dw, ln_g, ln_b, w_conv_out, w_att_out, w_o):
    T = h.shape[1]
    u, q, k, v, g_conv, g_att = mixer_inputs(h, pos, w_in)
    u_ext = jnp.concatenate([conv_cache.astype(u.dtype), u], axis=1)
    y_dw = causal_depthwise(u_ext, w_dw, b_dw)
    conv_state = u_ext[:, T:]
    outs, lses, kv_rows = [], [], []
    for gi, (win, dil) in enumerate(DIL_GROUPS):
        o, lse, kv_new = dilated_window_decode(q[:, :, gi], k[:, :, gi], v[:, :, gi], kv_caches[gi], dil, win // dil)
        outs.append(o)
        lses.append(lse)
        kv_rows.append(kv_new)
    y = mixer_output(y_dw, outs, lses, g_conv, g_att, ln_g, ln_b, w_conv_out, w_att_out, w_o)
    return y, kv_rows, conv_state


def swiglu(t, w_g, w_u, w_d):
    return (jax.nn.silu(t @ w_g) * (t @ w_u)) @ w_d


def routed_experts(t, idx, wts, w_eg, w_eu, w_ed):
    N, D = t.shape
    M = N * TOP_K
    e_flat = idx.reshape(M)
    tok_flat = jnp.repeat(jnp.arange(N, dtype=jnp.int32), TOP_K)
    w_flat = wts.reshape(M)
    order = jnp.argsort(e_flat)
    e_s, tok_s, w_s = e_flat[order], tok_flat[order], w_flat[order]
    counts = jnp.zeros((N_EXPERTS,), jnp.int32).at[e_flat].add(1)
    padded = (counts + MOE_BLOCK - 1) // MOE_BLOCK * MOE_BLOCK
    start = jnp.cumsum(counts) - counts
    pend = jnp.cumsum(padded)
    pstart = pend - padded
    dest = pstart[e_s] + jnp.arange(M, dtype=jnp.int32) - start[e_s]
    nb = (M + N_EXPERTS * (MOE_BLOCK - 1)) // MOE_BLOCK
    rows = nb * MOE_BLOCK
    row_tok = jnp.full((rows,), N, jnp.int32).at[dest].set(tok_s)
    row_w = jnp.zeros((rows,), t.dtype).at[dest].set(w_s)
    blk_e = jnp.minimum(jnp.searchsorted(pend, jnp.arange(nb, dtype=jnp.int32) * MOE_BLOCK, side='right'),
                        N_EXPERTS - 1)
    t_pad = jnp.concatenate([t, jnp.zeros((1, D), t.dtype)], axis=0)

    def expert_block(args):
        tok_b, w_b, e = args
        return swiglu(t_pad[tok_b], w_eg[e], w_eu[e], w_ed[e]) * w_b[:, None]

    ys = lax.map(expert_block, (row_tok.reshape(nb, MOE_BLOCK), row_w.reshape(nb, MOE_BLOCK), blk_e))
    return jax.ops.segment_sum(ys.reshape(rows, D), row_tok, num_segments=N + 1)[:N]


def moe_ffn(h, w_router, b_router, w_eg, w_eu, w_ed, w_sg, w_su, w_sd):
    B, S, D = h.shape
    t = h.reshape(B * S, D)
    scores = jax.nn.sigmoid((t @ w_router).astype(jnp.float32))
    sel = scores + b_router.astype(jnp.float32)
    per_group = N_EXPERTS // N_EXPERT_GROUPS
    grp_score = lax.top_k(sel.reshape(-1, N_EXPERT_GROUPS, per_group), 2)[0].sum(-1)
    _, top_g = lax.top_k(grp_score, TOPK_GROUPS)
    grp_mask = jax.nn.one_hot(top_g, N_EXPERT_GROUPS, dtype=jnp.float32).sum(1) > 0
    sel = jnp.where(jnp.repeat(grp_mask, per_group, axis=1), sel, -jnp.inf)
    _, idx = lax.top_k(sel, TOP_K)
    w = jnp.take_along_axis(scores, idx, axis=1)
    w = w / (jnp.sum(w, axis=-1, keepdims=True) + 1e-20) * ROUTED_SCALE
    out = routed_experts(t, idx, w.astype(t.dtype), w_eg, w_eu, w_ed) + swiglu(t, w_sg, w_su, w_sd)
    return out.reshape(B, S, D)


def setup_inputs(seed: int = 0) -> dict:
    key = jax.random.key(seed)
    ks = jax.random.split(key, 40)
    nrm = lambda k, shape, scale: scale * jax.random.normal(k, shape, jnp.float32)
    L = DEPTH
    kv_shape = lambda win: (L, DEC_BATCH, min(win, PAST_LEN), 2, HEADS_PER_GROUP, HEAD_DIM)
    return {
        'x_prompt': nrm(ks[0], (BATCH, SEQ, D_MODEL), 1.0),
        'x_sample': nrm(ks[1], (DEC_BATCH, DEC_SEQ, D_MODEL), 1.0),
        'cache_kv_w128': nrm(ks[2], kv_shape(DIL_GROUPS[0][0]), 1.0),
        'cache_kv_w512': nrm(ks[3], kv_shape(DIL_GROUPS[1][0]), 1.0),
        'cache_kv_w2048': nrm(ks[4], kv_shape(DIL_GROUPS[2][0]), 1.0),
        'state_conv': nrm(ks[5], (L, DEC_BATCH, CONV_WIDTH - 1, D_CONV), 1.0),
        'c_prompt': nrm(ks[6], (BATCH, D_MODEL), 1.0),
        'c_sample': nrm(ks[7], (DEC_BATCH, D_MODEL), 1.0),
        'g_mix': 1.0 + nrm(ks[8], (L, D_MODEL), 0.1),
        'w_ada_mix': nrm(ks[9], (L, D_MODEL, 3 * D_MODEL), 0.5 * D_MODEL ** -0.5),
        'b_ada_mix': nrm(ks[10], (L, 3 * D_MODEL), 0.02),
        'w_in': nrm(ks[11], (L, D_MODEL, D_IN), D_MODEL ** -0.5),
        'w_dw': nrm(ks[12], (L, CONV_WIDTH, D_CONV), CONV_WIDTH ** -0.5),
        'b_dw': nrm(ks[13], (L, D_CONV), 0.02),
        'ln_conv_g': 1.0 + nrm(ks[14], (L, D_CONV), 0.1),
        'ln_conv_b': nrm(ks[15], (L, D_CONV), 0.02),
        'w_conv_out': nrm(ks[16], (L, D_CONV, D_MODEL), D_CONV ** -0.5),
        'w_att_out': nrm(ks[17], (L, D_SLOT, D_MODEL), D_SLOT ** -0.5),
        'w_o': nrm(ks[18], (L, D_MODEL, D_MODEL), D_MODEL ** -0.5),
        'g_ffn': 1.0 + nrm(ks[19], (L, D_MODEL), 0.1),
        'w_ada_ffn': nrm(ks[20], (L, D_MODEL, 3 * D_MODEL), 0.5 * D_MODEL ** -0.5),
        'b_ada_ffn': nrm(ks[21], (L, 3 * D_MODEL), 0.02),
        'w_router': nrm(ks[22], (L, D_MODEL, N_EXPERTS), D_MODEL ** -0.5),
        'b_router': nrm(ks[23], (L, N_EXPERTS), 0.01),
        'w_exp_gate': nrm(ks[24], (L, N_EXPERTS, D_MODEL, D_EXPERT), D_MODEL ** -0.5),
        'w_exp_up': nrm(ks[25], (L, N_EXPERTS, D_MODEL, D_EXPERT), D_MODEL ** -0.5),
        'w_exp_down': nrm(ks[26], (L, N_EXPERTS, D_EXPERT, D_MODEL), D_EXPERT ** -0.5),
        'w_sh_gate': nrm(ks[27], (L, D_MODEL, D_EXPERT), D_MODEL ** -0.5),
        'w_sh_up': nrm(ks[28], (L, D_MODEL, D_EXPERT), D_MODEL ** -0.5),
        'w_sh_down': nrm(ks[29], (L, D_EXPERT, D_MODEL), D_EXPERT ** -0.5),
        'g_final': 1.0 + nrm(ks[30], (D_MODEL,), 0.1),
    }


def reference(x_prompt, x_sample, cache_kv_w128, cache_kv_w512, cache_kv_w2048, state_conv, c_prompt, c_sample,
              g_mix, w_ada_mix, b_ada_mix, w_in, w_dw, b_dw, ln_conv_g, ln_conv_b, w_conv_out, w_att_out, w_o,
              g_ffn, w_ada_ffn, b_ada_ffn, w_router, b_router, w_exp_gate, w_exp_up, w_exp_down,
              w_sh_gate, w_sh_up, w_sh_down, g_final):
    pos_p = jnp.arange(x_prompt.shape[1], dtype=jnp.int32)
    pos_s = PAST_LEN + jnp.arange(x_sample.shape[1], dtype=jnp.int32)
    kv_caches = (cache_kv_w128, cache_kv_w512, cache_kv_w2048)
    xp, xs = x_prompt, x_sample
    kv_p = [[] for _ in DIL_GROUPS]
    kv_s = [[] for _ in DIL_GROUPS]
    conv_p, conv_s = [], []
    for l in range(DEPTH):
        hp, gate_p = ada_modulate(xp, c_prompt, g_mix[l], w_ada_mix[l], b_ada_mix[l])
        hs, gate_s = ada_modulate(xs, c_sample, g_mix[l], w_ada_mix[l], b_ada_mix[l])
        yp, kvp_l, cp_l = mixer_prompt(hp, pos_p, w_in[l], w_dw[l], b_dw[l], ln_conv_g[l], ln_conv_b[l],
                                       w_conv_out[l], w_att_out[l], w_o[l])
        ys, kvs_l, cs_l = mixer_sample(hs, pos_s, [cache[l] for cache in kv_caches], state_conv[l],
                                       w_in[l], w_dw[l], b_dw[l], ln_conv_g[l], ln_conv_b[l],
                                       w_conv_out[l], w_att_out[l], w_o[l])
        xp = xp + gate_p * yp
        xs = xs + gate_s * ys
        for gi in range(N_DIL_GROUPS):
            kv_p[gi].append(kvp_l[gi])
            kv_s[gi].append(kvs_l[gi])
        conv_p.append(cp_l)
        conv_s.append(cs_l)
        hp, gate_p = ada_modulate(xp, c_prompt, g_ffn[l], w_ada_ffn[l], b_ada_ffn[l])
        hs, gate_s = ada_modulate(xs, c_sample, g_ffn[l], w_ada_ffn[l], b_ada_ffn[l])
        xp = xp + gate_p * moe_ffn(hp, w_router[l], b_router[l], w_exp_gate[l], w_exp_up[l], w_exp_down[l],
                                   w_sh_gate[l], w_sh_up[l], w_sh_down[l])
        xs = xs + gate_s * moe_ffn(hs, w_router[l], b_router[l], w_exp_gate[l], w_exp_up[l], w_exp_down[l],
                                   w_sh_gate[l], w_sh_up[l], w_sh_down[l])
    y_prompt = rms_norm(xp, g_final)
    y_sample = rms_norm(xs, g_final)
    return (y_prompt, y_sample,
            jnp.stack(kv_p[0]), jnp.stack(kv_p[1]), jnp.stack(kv_p[2]), jnp.stack(conv_p),
            jnp.stack(kv_s[0]), jnp.stack(kv_s[1]), jnp.stack(kv_s[2]), jnp.stack(conv_s))
```

```python
import functools

import jax
import jax.numpy as jnp
from jax import lax
from jax.experimental import pallas as pl
from jax.experimental.pallas import tpu as pltpu
from jax.experimental.pallas import tpu_sc as plsc

F32 = jnp.float32
BF16 = jnp.bfloat16
I32 = jnp.int32

D_MODEL = 1024
D_CONV = 1024
CONV_WIDTH = 31
HEAD_DIM = 64
HEADS = 8
D_SLOT = HEADS * HEAD_DIM
DILATIONS = (1, 4, 16)
N_GROUPS = 3
D_ATT = N_GROUPS * D_SLOT
Q_BLOCK = 128
ROPE_THETA = 10000.0
N_EXPERTS = 256
TOP_K = 8
N_EXPERT_GROUPS = 8
TOPK_GROUPS = 4
GROUP_SIZE = N_EXPERTS // N_EXPERT_GROUPS
D_EXPERT = 256
ROUTED_SCALE = 2.5
RMS_EPS = 1e-6
LN_EPS = 1e-5

COL_A = 0
COL_B = D_CONV
COL_Q = 2 * D_CONV
COL_K = COL_Q + D_ATT
COL_V = COL_K + D_ATT
COL_GC = COL_V + D_ATT
COL_GA = COL_GC + D_MODEL
D_IN = COL_GA + D_MODEL

LANES = 128
CONV_HALO = 32
TOKEN_TILE = 512
EXPERT_BLOCK = 256
ROW_CHUNKS = 4
CHUNK = D_MODEL // ROW_CHUNKS
SC_WORKERS = 32
SC_WINDOW = 128
NEG_BIG = -1e30
VMEM_LIMIT = 56 * 1024 * 1024


def _sigmoid(x):
    return 1.0 / (1.0 + jnp.exp(-x))


def _const_spec(shape):
    nd = len(shape)
    return pl.BlockSpec(shape, lambda *_: (0,) * nd, pipeline_mode=pl.Buffered(1))


def _params(sem):
    return pltpu.CompilerParams(dimension_semantics=sem, vmem_limit_bytes=VMEM_LIMIT)


def _ada_kernel(c_ref, w_ref, b_ref, o_ref):
    c = c_ref[...]
    s = (c * _sigmoid(c)).astype(BF16)
    o_ref[...] = jnp.dot(s, w_ref[...].astype(BF16), preferred_element_type=F32) + b_ref[...]


def _ada(c_all, w, b):
    rows = c_all.shape[0]
    cols = w.shape[1]
    tn = 768
    return pl.pallas_call(
        _ada_kernel,
        grid=(cols // tn,),
        in_specs=[pl.BlockSpec((rows, D_MODEL), lambda j: (0, 0)),
                  pl.BlockSpec((D_MODEL, tn), lambda j: (0, j)),
                  pl.BlockSpec((1, tn), lambda j: (0, j))],
        out_specs=pl.BlockSpec((rows, tn), lambda j: (0, j)),
        out_shape=jax.ShapeDtypeStruct((rows, cols), F32),
        compiler_params=_params(("arbitrary",)),
        name="ada",
    )(c_all, w, b.reshape(1, cols))


def _modulated_norm(x, g, mod):
    shift = mod[:, 0:D_MODEL]
    scale = mod[:, D_MODEL:2 * D_MODEL]
    ms = jnp.mean(x * x, axis=-1, keepdims=True)
    return (x * lax.rsqrt(ms + RMS_EPS)) * g * (1.0 + scale) + shift


def _rope(t, cos, sin_signed):
    lane = lax.broadcasted_iota(I32, (t.shape[0], LANES), 1)
    first_half = (lane & (HEAD_DIM - 1)) < (HEAD_DIM // 2)
    outs = []
    for j in range(D_SLOT // LANES):
        ch = t[:, j * LANES:(j + 1) * LANES]
        rot = jnp.where(first_half, pltpu.roll(ch, LANES - HEAD_DIM // 2, 1), pltpu.roll(ch, HEAD_DIM // 2, 1))
        outs.append(ch * cos + rot * sin_signed)
    return jnp.concatenate(outs, axis=1)


def _layer_norm_swish(y, g, b):
    mu = jnp.mean(y, axis=-1, keepdims=True)
    yc = y - mu
    var = jnp.mean(yc * yc, axis=-1, keepdims=True)
    z = yc * lax.rsqrt(var + LN_EPS) * g + b
    return z * _sigmoid(z)


def _inproj_kernel(x_ref, mod_ref, g_ref, w_ref, c0_ref, s0_ref, c1_ref, s1_ref, c2_ref, s2_ref,
                   wdw_ref, bdw_ref, lng_ref, lnb_ref, wco_ref,
                   q0_ref, k0_ref, v0_ref, q1_ref, k1_ref, v1_ref, q2_ref, k2_ref, v2_ref,
                   gc_ref, ga_ref, ut_ref,
                   hf_ref, hb0_ref, hb1_ref, hb2_ref, ubuf_ref, *, tm):
    i = pl.program_id(1)
    cw = 512

    h = _modulated_norm(x_ref[...], g_ref[...], mod_ref[...])
    nlc = D_MODEL // LANES
    for c in range(nlc):
        hf_ref[c] = h[:, c * LANES:(c + 1) * LANES]
    hb0_ref[...] = h.astype(BF16)
    for d, hb in ((DILATIONS[1], hb1_ref), (DILATIONS[2], hb2_ref)):
        n = tm // d
        for r in range(d):
            for c in range(nlc):
                hb[r * n:(r + 1) * n, c * LANES:(c + 1) * LANES] = hf_ref[c, pl.ds(r, n, stride=d), :].astype(BF16)

    def mm(hb, col, width=cw):
        return jnp.dot(hb[...], w_ref[:, col:col + width], preferred_element_type=F32)

    @pl.when(i == 0)
    def _():
        ubuf_ref[0:CONV_HALO, :] = jnp.zeros((CONV_HALO, D_CONV), F32)

    for c in range(0, D_CONV, cw):
        a = mm(hb0_ref, COL_A + c)
        b = mm(hb0_ref, COL_B + c)
        ubuf_ref[CONV_HALO:CONV_HALO + tm, c:c + cw] = a * _sigmoid(b)

    groups = ((hb0_ref, 1, c0_ref, s0_ref, q0_ref, k0_ref, v0_ref),
              (hb1_ref, DILATIONS[1], c1_ref, s1_ref, q1_ref, k1_ref, v1_ref),
              (hb2_ref, DILATIONS[2], c2_ref, s2_ref, q2_ref, k2_ref, v2_ref))
    for gi, (hb, d, c_ref, s_ref, q_ref, k_ref, v_ref) in enumerate(groups):
        cos = c_ref[...]
        sin = s_ref[...]
        n = tm // d

        def put(ref, val):
            vb = val.astype(BF16)
            if d == 1:
                ref[...] = vb
            else:
                for r in range(d):
                    ref[r] = vb[r * n:(r + 1) * n, :]

        put(q_ref, _rope(mm(hb, COL_Q + gi * D_SLOT), cos, sin) * (HEAD_DIM ** -0.5))
        put(k_ref, _rope(mm(hb, COL_K + gi * D_SLOT), cos, sin))
        put(v_ref, mm(hb, COL_V + gi * D_SLOT))

    for c in range(0, D_MODEL, cw):
        ga_ref[:, c:c + cw] = _sigmoid(mm(hb0_ref, COL_GA + c)).astype(BF16)

    rb = 32
    first_tap = CONV_HALO - (CONV_WIDTH - 1)
    win = rb + CONV_HALO

    def conv_rows(r, carry):
        r0 = pl.multiple_of(r * rb, rb)
        for c in range(nlc):
            cs = slice(c * LANES, (c + 1) * LANES)
            window = ubuf_ref[pl.ds(r0, win), cs]
            acc = jnp.broadcast_to(bdw_ref[:, cs], (rb, LANES))
            for k in range(CONV_WIDTH):
                acc = acc + wdw_ref[k:k + 1, cs] * window[first_tap + k:first_tap + k + rb, :]
            hf_ref[c, pl.ds(r0, rb), :] = acc
        return carry

    lax.fori_loop(0, tm // rb, conv_rows, 0)

    y_dw = jnp.concatenate([hf_ref[c] for c in range(nlc)], axis=1)
    z = _layer_norm_swish(y_dw, lng_ref[...], lnb_ref[...]).astype(BF16)
    for c in range(0, D_MODEL, cw):
        co = jnp.dot(z, wco_ref[:, c:c + cw], preferred_element_type=F32)
        gc_ref[:, c:c + cw] = (_sigmoid(mm(hb0_ref, COL_GC + c)) * co).astype(BF16)

    tail = ubuf_ref[tm:tm + CONV_HALO, :]
    ut_ref[...] = tail
    ubuf_ref[0:CONV_HALO, :] = tail


def _inproj_prompt(x, mod, g, w_in_b, tables, w_dw, b_dw, ln_g, ln_b, w_co_b, tm):
    B, S, _ = x.shape
    nt = S // tm
    d1, d2 = DILATIONS[1], DILATIONS[2]
    tok = lambda b, i: (b, i, 0)
    tab = pl.BlockSpec((tm, LANES), lambda b, i: (i, 0))
    in_specs = [
        pl.BlockSpec((None, tm, D_MODEL), tok),
        pl.BlockSpec((None, 1, 3 * D_MODEL), lambda b, i: (b, 0, 0)),
        _const_spec((1, D_MODEL)),
        _const_spec((D_MODEL, D_IN)),
        tab, tab, tab, tab, tab, tab,
        _const_spec((CONV_WIDTH, D_CONV)),
        _const_spec((1, D_CONV)),
        _const_spec((1, D_CONV)),
        _const_spec((1, D_CONV)),
        _const_spec((D_CONV, D_MODEL)),
    ]
    nat = pl.BlockSpec((None, tm, D_SLOT), tok)
    st1 = pl.BlockSpec((None, d1, tm // d1, D_SLOT), lambda b, i: (b, 0, i, 0))
    st2 = pl.BlockSpec((None, d2, tm // d2, D_SLOT), lambda b, i: (b, 0, i, 0))
    out_specs = [nat, nat, nat, st1, st1, st1, st2, st2, st2,
                 pl.BlockSpec((None, tm, D_MODEL), tok),
                 pl.BlockSpec((None, tm, D_MODEL), tok),
                 pl.BlockSpec((None, CONV_HALO, D_CONV), lambda b, i: (b, 0, 0))]
    s0 = jax.ShapeDtypeStruct((B, S, D_SLOT), BF16)
    s1 = jax.ShapeDtypeStruct((B, d1, S // d1, D_SLOT), BF16)
    s2 = jax.ShapeDtypeStruct((B, d2, S // d2, D_SLOT), BF16)
    out_shape = [s0, s0, s0, s1, s1, s1, s2, s2, s2,
                 jax.ShapeDtypeStruct((B, S, D_MODEL), BF16),
                 jax.ShapeDtypeStruct((B, S, D_MODEL), BF16),
                 jax.ShapeDtypeStruct((B, CONV_HALO, D_CONV), F32)]
    scratch = [pltpu.VMEM((D_MODEL // LANES, tm, LANES), F32),
               pltpu.VMEM((tm, D_MODEL), BF16),
               pltpu.VMEM((tm, D_MODEL), BF16),
               pltpu.VMEM((tm, D_MODEL), BF16),
               pltpu.VMEM((tm + CONV_HALO, D_CONV), F32)]
    return pl.pallas_call(
        functools.partial(_inproj_kernel, tm=tm),
        grid=(B, nt),
        in_specs=in_specs, out_specs=out_specs, out_shape=out_shape,
        scratch_shapes=scratch,
        compiler_params=_params(("arbitrary", "arbitrary")),
        name="inproj",
    )(x, mod, g, w_in_b, *tables, w_dw, b_dw, ln_g, ln_b, w_co_b)


def _attn_kernel(q_ref, kp_ref, kc_ref, vp_ref, vc_ref, o_ref, lse_ref):
    j = pl.program_id(1)
    qi = lax.broadcasted_iota(I32, (Q_BLOCK, Q_BLOCK), 0)
    ki = lax.broadcasted_iota(I32, (Q_BLOCK, Q_BLOCK), 1)
    mask_c = ki <= qi
    mask_p = jnp.logical_and(ki >= qi, j > 0)
    lane = lax.broadcasted_iota(I32, (Q_BLOCK, LANES), 1)
    nt = (((1,), (1,)), ((), ()))
    lse_all = jnp.zeros((Q_BLOCK, LANES), F32)
    for h in range(HEADS):
        sl = slice(h * HEAD_DIM, (h + 1) * HEAD_DIM)
        q = q_ref[:, sl]
        sp = jnp.where(mask_p, lax.dot_general(q, kp_ref[:, sl], nt, preferred_element_type=F32), NEG_BIG)
        sc = jnp.where(mask_c, lax.dot_general(q, kc_ref[:, sl], nt, preferred_element_type=F32), NEG_BIG)
        m = jnp.maximum(jnp.max(sp, axis=-1, keepdims=True), jnp.max(sc, axis=-1, keepdims=True))
        pp = jnp.exp(sp - m)
        pc = jnp.exp(sc - m)
        den = jnp.sum(pp, axis=-1, keepdims=True) + jnp.sum(pc, axis=-1, keepdims=True)
        o = (jnp.dot(pp.astype(BF16), vp_ref[:, sl], preferred_element_type=F32)
             + jnp.dot(pc.astype(BF16), vc_ref[:, sl], preferred_element_type=F32))
        o_ref[:, sl] = (o / den).astype(BF16)
        lse_all = jnp.where(lane == h, m + jnp.log(den), lse_all)
    lse_ref[...] = lse_all


def _attn_prompt(q, k, v):
    ns, L, _ = q.shape
    nb = L // Q_BLOCK
    cur = pl.BlockSpec((None, Q_BLOCK, D_SLOT), lambda n, j: (n, j, 0))
    prev = pl.BlockSpec((None, Q_BLOCK, D_SLOT), lambda n, j: (n, jnp.maximum(j - 1, 0), 0))
    return pl.pallas_call(
        _attn_kernel,
        grid=(ns, nb),
        in_specs=[cur, prev, cur, prev, cur],
        out_specs=[cur, pl.BlockSpec((None, Q_BLOCK, LANES), lambda n, j: (n, j, 0))],
        out_shape=[jax.ShapeDtypeStruct((ns, L, D_SLOT), BF16),
                   jax.ShapeDtypeStruct((ns, L, LANES), F32)],
        compiler_params=_params(("arbitrary", "arbitrary")),
        name="attn",
    )(q, k, k, v, v)


def _first_index_of_max(x, lane, width):
    m = jnp.max(x, axis=-1, keepdims=True)
    idx = jnp.min(jnp.where(x == m, lane, width), axis=-1, keepdims=True)
    return m, idx


def _route(logits, b_router, carry, tm):
    ninf = -jnp.inf
    scores = _sigmoid(logits)
    sel = scores + b_router
    lane = lax.broadcasted_iota(I32, (tm, N_EXPERTS), 1)
    grp = lane // GROUP_SIZE
    gs_cols = []
    gs_lane = jnp.zeros((tm, N_EXPERTS), F32)
    for g in range(N_EXPERT_GROUPS):
        in_g = grp == g
        xg = jnp.where(in_g, sel, ninf)
        m1, i1 = _first_index_of_max(xg, lane, N_EXPERTS)
        m2 = jnp.max(jnp.where(lane == i1, ninf, xg), axis=-1, keepdims=True)
        gs = m1 + m2
        gs_cols.append(gs)
        gs_lane = jnp.where(in_g, gs, gs_lane)
    beaten = jnp.zeros((tm, N_EXPERTS), I32)
    for g in range(N_EXPERT_GROUPS):
        s = gs_cols[g]
        better = jnp.logical_or(s > gs_lane, jnp.logical_and(s == gs_lane, g < grp))
        beaten = beaten + better.astype(I32)
    selm = jnp.where(beaten < TOPK_GROUPS, sel, ninf)

    idx_cols, w_cols = [], []
    picked = jnp.zeros((tm, N_EXPERTS), F32)
    for _ in range(TOP_K):
        _, ik = _first_index_of_max(selm, lane, N_EXPERTS)
        hit = lane == ik
        w_cols.append(jnp.sum(jnp.where(hit, scores, 0.0), axis=-1, keepdims=True))
        selm = jnp.where(hit, ninf, selm)
        picked = jnp.where(hit, 1.0, picked)
        idx_cols.append(ik)
    wsum = w_cols[0]
    for wk in w_cols[1:]:
        wsum = wsum + wk
    denom = wsum + 1e-20

    ri = lax.broadcasted_iota(I32, (tm, tm), 0)
    ci = lax.broadcasted_iota(I32, (tm, tm), 1)
    tri = jnp.where(ci < ri, 1.0, 0.0).astype(BF16)
    before = jnp.dot(tri, picked.astype(BF16), preferred_element_type=F32) + carry
    new_carry = carry + jnp.sum(picked, axis=0, keepdims=True)

    lane_o = lax.broadcasted_iota(I32, (tm, LANES), 1)
    idx_o = jnp.zeros((tm, LANES), I32)
    w_o = jnp.zeros((tm, LANES), F32)
    pos_o = jnp.zeros((tm, LANES), I32)
    for k in range(TOP_K):
        pk = jnp.sum(jnp.where(lane == idx_cols[k], before, 0.0), axis=-1, keepdims=True)
        idx_o = jnp.where(lane_o == k, idx_cols[k], idx_o)
        w_o = jnp.where(lane_o == k, w_cols[k] / denom * ROUTED_SCALE, w_o)
        pos_o = jnp.where(lane_o == k, pk.astype(I32), pos_o)
    return idx_o, w_o, pos_o, new_carry


def _post_kernel(*refs, tm, combine):
    refs = list(refs)
    x_ref, modm_ref = refs[0:2]
    p = 2
    if combine:
        o0_ref, o1_ref, o2_ref, l0_ref, l1_ref, l2_ref = refs[p:p + 6]
        p += 6
    else:
        att_ref = refs[p]
        p += 1
    (gc_ref, ga_ref, wao_ref, wo_ref, gffn_ref, modf_ref, wr_ref, br_ref,
     wsg_ref, wsu_ref, wsd_ref, cin_ref) = refs[p:p + 12]
    p += 12
    h2_ref, x2_ref, idx_ref, wts_ref, pos_ref, cnt_ref = refs[p:p + 6]
    p += 6
    if combine:
        o1n_ref, o2n_ref, l1n_ref, l2n_ref = refs[p:p + 4]

    first = jnp.logical_and(pl.program_id(0) == 0, pl.program_id(1) == 0)

    @pl.when(first)
    def _():
        cnt_ref[...] = cin_ref[...]

    if combine:
        nsc = D_SLOT // LANES
        for d, o_ref, l_ref, on_ref, ln_ref in ((DILATIONS[1], o1_ref, l1_ref, o1n_ref, l1n_ref),
                                                (DILATIONS[2], o2_ref, l2_ref, o2n_ref, l2n_ref)):
            n = tm // d
            for r in range(d):
                ln_ref[pl.ds(r, n, stride=d), :] = l_ref[r]
                orow = o_ref[r].astype(F32)
                for c in range(nsc):
                    on_ref[c, pl.ds(r, n, stride=d), :] = orow[:, c * LANES:(c + 1) * LANES]
        l0 = l0_ref[...]
        l1 = l1n_ref[...]
        l2 = l2n_ref[...]
        mx = jnp.maximum(jnp.maximum(l0, l1), l2)
        e0 = jnp.exp(l0 - mx)
        e1 = jnp.exp(l1 - mx)
        e2 = jnp.exp(l2 - mx)
        esum = e0 + e1 + e2
        head = lax.broadcasted_iota(I32, (tm, D_SLOT), 1) // HEAD_DIM

        def expand(w):
            out = jnp.zeros((tm, D_SLOT), F32)
            for h in range(HEADS):
                out = jnp.where(head == h, w[:, h:h + 1], out)
            return out

        o1n = jnp.concatenate([o1n_ref[c] for c in range(nsc)], axis=1)
        o2n = jnp.concatenate([o2n_ref[c] for c in range(nsc)], axis=1)
        att = (expand(e0 / esum) * o0_ref[...].astype(F32) + expand(e1 / esum) * o1n
               + expand(e2 / esum) * o2n)
    else:
        att = att_ref[...]

    att_out = jnp.dot(att.astype(BF16), wao_ref[...], preferred_element_type=F32)
    merged = gc_ref[...].astype(F32) + ga_ref[...].astype(F32) * att_out
    y = jnp.dot(merged.astype(BF16), wo_ref[...], preferred_element_type=F32)
    x1 = x_ref[...] + modm_ref[:, 2 * D_MODEL:3 * D_MODEL] * y

    modf = modf_ref[...]
    h2 = _modulated_norm(x1, gffn_ref[...], modf)
    h2b = h2.astype(BF16)
    for c in range(ROW_CHUNKS):
        h2_ref[c] = h2[:, c * CHUNK:(c + 1) * CHUNK]
    sg = jnp.dot(h2b, wsg_ref[...], preferred_element_type=F32)
    su = jnp.dot(h2b, wsu_ref[...], preferred_element_type=F32)
    sh = jnp.dot((sg * _sigmoid(sg) * su).astype(BF16), wsd_ref[...], preferred_element_type=F32)
    x2_ref[...] = x1 + modf[:, 2 * D_MODEL:3 * D_MODEL] * sh

    logits = jnp.dot(h2, wr_ref[...], preferred_element_type=F32, precision=lax.Precision.HIGHEST)
    idx_o, w_o, pos_o, new_carry = _route(logits, br_ref[...], cnt_ref[...], tm)
    idx_ref[...] = idx_o
    wts_ref[...] = w_o
    pos_ref[...] = pos_o
    cnt_ref[...] = new_carry


def _post(x, modm, attn_inputs, gc, ga, w_ao_b, w_o_b, g_ffn, modf, w_router, b_router,
          wsg_b, wsu_b, wsd_b, carry_in, tm, combine):
    B, S, _ = x.shape
    nt = S // tm
    tok = lambda b, i: (b, i, 0)
    mod_rows = modm.shape[1]
    mod_spec = (pl.BlockSpec((None, 1, 3 * D_MODEL), lambda b, i: (b, 0, 0)) if mod_rows == 1
                else pl.BlockSpec((None, tm, 3 * D_MODEL), tok))
    full = pl.BlockSpec((None, tm, D_MODEL), tok)
    in_specs = [full, mod_spec]
    scratch = []
    if combine:
        d1, d2 = DILATIONS[1], DILATIONS[2]
        in_specs += [pl.BlockSpec((None, tm, D_SLOT), tok),
                     pl.BlockSpec((None, d1, tm // d1, D_SLOT), lambda b, i: (b, 0, i, 0)),
                     pl.BlockSpec((None, d2, tm // d2, D_SLOT), lambda b, i: (b, 0, i, 0)),
                     pl.BlockSpec((None, tm, LANES), tok),
                     pl.BlockSpec((None, d1, tm // d1, LANES), lambda b, i: (b, 0, i, 0)),
                     pl.BlockSpec((None, d2, tm // d2, LANES), lambda b, i: (b, 0, i, 0))]
        scratch = [pltpu.VMEM((D_SLOT // LANES, tm, LANES), F32),
                   pltpu.VMEM((D_SLOT // LANES, tm, LANES), F32),
                   pltpu.VMEM((tm, LANES), F32),
                   pltpu.VMEM((tm, LANES), F32)]
    else:
        in_specs += [pl.BlockSpec((None, tm, D_SLOT), tok)]
    in_specs += [full, full,
                 _const_spec((D_SLOT, D_MODEL)), _const_spec((D_MODEL, D_MODEL)),
                 _const_spec((1, D_MODEL)), mod_spec,
                 _const_spec((D_MODEL, N_EXPERTS)), _const_spec((1, N_EXPERTS)),
                 _const_spec((D_MODEL, D_EXPERT)), _const_spec((D_MODEL, D_EXPERT)),
                 _const_spec((D_EXPERT, D_MODEL)), _const_spec((1, N_EXPERTS))]
    pack = pl.BlockSpec((None, tm, LANES), tok)
    chunked = pl.BlockSpec((ROW_CHUNKS, tm, CHUNK), lambda b, i: (0, b * nt + i, 0))
    out_specs = [chunked, full, pack, pack, pack, pl.BlockSpec((1, N_EXPERTS), lambda b, i: (0, 0))]
    out_shape = [jax.ShapeDtypeStruct((ROW_CHUNKS, B * S, CHUNK), F32),
                 jax.ShapeDtypeStruct((B, S, D_MODEL), F32),
                 jax.ShapeDtypeStruct((B, S, LANES), I32),
                 jax.ShapeDtypeStruct((B, S, LANES), F32),
                 jax.ShapeDtypeStruct((B, S, LANES), I32),
                 jax.ShapeDtypeStruct((1, N_EXPERTS), F32)]
    return pl.pallas_call(
        functools.partial(_post_kernel, tm=tm, combine=combine),
        grid=(B, nt),
        in_specs=in_specs, out_specs=out_specs, out_shape=out_shape,
        scratch_shapes=scratch,
        compiler_params=_params(("arbitrary", "arbitrary")),
        name="post_prompt" if combine else "post_sample",
    )(x, modm, *attn_inputs, gc, ga, w_ao_b, w_o_b, g_ffn, modf, w_router, b_router,
      wsg_b, wsu_b, wsd_b, carry_in)


def _inproj_sample_kernel(x_ref, mod_ref, g_ref, w_ref, cos_ref, sin_ref,
                          u_ref, q_ref, k_ref, v_ref, sgc_ref, ga_ref):
    cw = 512
    hb = _modulated_norm(x_ref[...], g_ref[...], mod_ref[...]).astype(BF16)

    def mm(col, width=cw):
        return jnp.dot(hb, w_ref[:, col:col + width], preferred_element_type=F32)

    cos = cos_ref[...]
    sin = sin_ref[...]
    for c in range(0, D_CONV, cw):
        u_ref[:, c:c + cw] = mm(COL_A + c) * _sigmoid(mm(COL_B + c))
        sgc_ref[:, c:c + cw] = _sigmoid(mm(COL_GC + c))
        ga_ref[:, c:c + cw] = _sigmoid(mm(COL_GA + c)).astype(BF16)
    for gi in range(N_GROUPS):
        cs = slice(gi * D_SLOT, (gi + 1) * D_SLOT)
        q_ref[:, cs] = _rope(mm(COL_Q + gi * D_SLOT), cos, sin) * (HEAD_DIM ** -0.5)
        k_ref[:, cs] = _rope(mm(COL_K + gi * D_SLOT), cos, sin)
        v_ref[:, cs] = mm(COL_V + gi * D_SLOT)


def _inproj_sample(x, mod, g, w_in_b, cos, sin):
    ns = x.shape[0]
    whole = lambda shape: pl.BlockSpec(shape, lambda i: (0,) * len(shape))
    f = lambda cols, dt=F32: jax.ShapeDtypeStruct((ns, cols), dt)
    return pl.pallas_call(
        _inproj_sample_kernel,
        grid=(1,),
        in_specs=[whole((ns, D_MODEL)), whole((ns, 3 * D_MODEL)), _const_spec((1, D_MODEL)),
                  _const_spec((D_MODEL, D_IN)), whole((1, LANES)), whole((1, LANES))],
        out_specs=[whole((ns, D_CONV)), whole((ns, D_ATT)), whole((ns, D_ATT)), whole((ns, D_ATT)),
                   whole((ns, D_MODEL)), whole((ns, D_MODEL))],
        out_shape=[f(D_CONV), f(D_ATT), f(D_ATT), f(D_ATT), f(D_MODEL), f(D_MODEL, BF16)],
        compiler_params=_params(("arbitrary",)),
        name="inproj_sample",
    )(x, mod, g, w_in_b, cos, sin)


def _sample_mix_kernel(state_ref, u_ref, sgc_ref, wdw_ref, bdw_ref, lng_ref, lnb_ref, wco_ref,
                       o0_ref, o1_ref, o2_ref, l0_ref, l1_ref, l2_ref,
                       gc_ref, att_ref, cst_ref):
    nprev = CONV_WIDTH - 1
    u = u_ref[...]
    y = wdw_ref[nprev:CONV_WIDTH, :] * u + bdw_ref[...]
    for k in range(nprev):
        y = y + wdw_ref[k:k + 1, :] * state_ref[k]
    z = _layer_norm_swish(y, lng_ref[...], lnb_ref[...]).astype(BF16)
    co = jnp.dot(z, wco_ref[...], preferred_element_type=F32)
    gc_ref[...] = (sgc_ref[...] * co).astype(BF16)
    for k in range(nprev - 1):
        cst_ref[k] = state_ref[k + 1]
    cst_ref[nprev - 1] = u

    l0 = l0_ref[...]
    l1 = l1_ref[...]
    l2 = l2_ref[...]
    mx = jnp.maximum(jnp.maximum(l0, l1), l2)
    e0 = jnp.exp(l0 - mx)
    e1 = jnp.exp(l1 - mx)
    e2 = jnp.exp(l2 - mx)
    esum = e0 + e1 + e2
    att_ref[...] = (e0 / esum) * o0_ref[...] + (e1 / esum) * o1_ref[...] + (e2 / esum) * o2_ref[...]


def _sample_mix(state, u, sgc, w_dw, b_dw, ln_g, ln_b, w_co_b, outs, lses):
    ns = state.shape[1]
    nprev = CONV_WIDTH - 1
    whole = lambda *shape: pl.BlockSpec(shape, lambda i: (0,) * len(shape))
    slot = whole(ns, D_SLOT)
    return pl.pallas_call(
        _sample_mix_kernel,
        grid=(1,),
        in_specs=[whole(nprev, ns, D_CONV), whole(ns, D_CONV), whole(ns, D_MODEL),
                  whole(CONV_WIDTH, D_CONV), whole(1, D_CONV), whole(1, D_CONV), whole(1, D_CONV),
                  whole(D_CONV, D_MODEL), slot, slot, slot, slot, slot, slot],
        out_specs=[whole(ns, D_MODEL), slot, whole(nprev, ns, D_CONV)],
        out_shape=[jax.ShapeDtypeStruct((ns, D_MODEL), BF16),
                   jax.ShapeDtypeStruct((ns, D_SLOT), F32),
                   jax.ShapeDtypeStruct((nprev, ns, D_CONV), F32)],
        compiler_params=_params(("arbitrary",)),
        name="sample_mix",
    )(state, u, sgc, w_dw, b_dw, ln_g, ln_b, w_co_b, *outs, *lses)


def _sample_cache_kernel(c_ref, q_ref, kn_ref, vn_ref, co_ref, o_ref, l_ref, *, hb, n, dil):
    lane = lax.broadcasted_iota(I32, (1, n), 1)
    back = n - lane
    use = jnp.logical_and(jnp.logical_and(back % dil == 0, back <= Q_BLOCK * dil), back >= dil)
    last = lax.broadcasted_iota(I32, (HEAD_DIM, n), 1) == n - 1
    for h in range(hb):
        kc = c_ref[0, h]
        vc = c_ref[1, h]
        q = q_ref[h]
        kn = kn_ref[h]
        vn = vn_ref[h]
        s = jnp.where(use, jnp.sum(kc * q, axis=0, keepdims=True), NEG_BIG)
        sn = jnp.sum(kn * q, axis=0, keepdims=True)
        m = jnp.maximum(jnp.max(s, axis=1, keepdims=True), sn)
        p = jnp.exp(s - m)
        pn = jnp.exp(sn - m)
        den = jnp.sum(p, axis=1, keepdims=True) + pn
        o_ref[h] = (jnp.sum(vc * p, axis=1, keepdims=True) + vn * pn) / den
        l_ref[h] = jnp.broadcast_to(m + jnp.log(den), (HEAD_DIM, 1))
        co_ref[0, h] = jnp.where(last, kn, pltpu.roll(kc, n - 1, 1))
        co_ref[1, h] = jnp.where(last, vn, pltpu.roll(vc, n - 1, 1))


def _sample_cache(cache_t, q, kn, vn, gi, dil):
    ns, _, _, _, n = cache_t.shape
    hb = HEADS if n * HEAD_DIM * HEADS * 2 * 4 <= 4 * 1024 * 1024 else HEADS // 2
    cspec = pl.BlockSpec((None, 2, hb, HEAD_DIM, n), lambda b, h: (b, 0, h, 0, 0))
    vspec = pl.BlockSpec((None, None, hb, HEAD_DIM, 1), lambda b, h: (b, gi, h, 0, 0))
    ospec = pl.BlockSpec((None, hb, HEAD_DIM, 1), lambda b, h: (b, h, 0, 0))
    col = jax.ShapeDtypeStruct((ns, HEADS, HEAD_DIM, 1), F32)
    return pl.pallas_call(
        functools.partial(_sample_cache_kernel, hb=hb, n=n, dil=dil),
        grid=(ns, HEADS // hb),
        in_specs=[cspec, vspec, vspec, vspec],
        out_specs=[cspec, ospec, ospec],
        out_shape=[jax.ShapeDtypeStruct(cache_t.shape, F32), col, col],
        compiler_params=_params(("arbitrary", "arbitrary")),
        name="sample_cache",
    )(cache_t, q, kn, vn)


def _experts_kernel(be_ref, bv_ref, x_ref, wg_ref, wu_ref, wd_ref, y_ref, wgb_ref, wub_ref, wdb_ref, *, bm):
    i = pl.program_id(0)
    e = be_ref[i]
    prev = be_ref[jnp.maximum(i - 1, 0)]
    valid = bv_ref[i]

    @pl.when(jnp.logical_or(i == 0, e != prev))
    def _():
        wgb_ref[...] = wg_ref[...].astype(BF16)
        wub_ref[...] = wu_ref[...].astype(BF16)
        wdb_ref[...] = wd_ref[...].astype(BF16)

    @pl.when(valid > 0)
    def _():
        row = lax.broadcasted_iota(I32, (bm, D_MODEL), 0)
        xf = jnp.concatenate([x_ref[c] for c in range(ROW_CHUNKS)], axis=1)
        x = jnp.where(row < valid, xf, 0.0).astype(BF16)
        g = jnp.dot(x, wgb_ref[...], preferred_element_type=F32)
        u = jnp.dot(x, wub_ref[...], preferred_element_type=F32)
        a = (g * _sigmoid(g) * u).astype(BF16)
        y = jnp.dot(a, wdb_ref[...], preferred_element_type=F32)
        for c in range(ROW_CHUNKS):
            y_ref[c] = y[:, c * CHUNK:(c + 1) * CHUNK]

    @pl.when(valid <= 0)
    def _():
        y_ref[...] = jnp.zeros((ROW_CHUNKS, bm, CHUNK), F32)


def _experts(blk_e, blk_valid, x_sorted, w_g, w_u, w_d, bm):
    rows = x_sorted.shape[1]
    nblk = rows // bm
    grid_spec = pltpu.PrefetchScalarGridSpec(
        num_scalar_prefetch=2,
        grid=(nblk,),
        in_specs=[pl.BlockSpec((ROW_CHUNKS, bm, CHUNK), lambda i, be, bv: (0, i, 0)),
                  pl.BlockSpec((None, D_MODEL, D_EXPERT), lambda i, be, bv: (be[i], 0, 0)),
                  pl.BlockSpec((None, D_MODEL, D_EXPERT), lambda i, be, bv: (be[i], 0, 0)),
                  pl.BlockSpec((None, D_EXPERT, D_MODEL), lambda i, be, bv: (be[i], 0, 0))],
        out_specs=pl.BlockSpec((ROW_CHUNKS, bm, CHUNK), lambda i, be, bv: (0, i, 0)),
        scratch_shapes=[pltpu.VMEM((D_MODEL, D_EXPERT), BF16), pltpu.VMEM((D_MODEL, D_EXPERT), BF16),
                        pltpu.VMEM((D_EXPERT, D_MODEL), BF16)])
    return pl.pallas_call(
        functools.partial(_experts_kernel, bm=bm),
        grid_spec=grid_spec,
        out_shape=jax.ShapeDtypeStruct((ROW_CHUNKS, rows, CHUNK), F32),
        compiler_params=_params(("arbitrary",)),
        name="experts",
    )(blk_e, blk_valid, x_sorted, w_g, w_u, w_d)


def _final_kernel(x2_ref, modf_ref, wts_ref, yg_ref, gfin_ref, o_ref):
    w = wts_ref[...]

    def picked(k):
        return jnp.concatenate([yg_ref[k, c] for c in range(ROW_CHUNKS)], axis=1)

    routed = w[:, 0:1] * picked(0)
    for k in range(1, TOP_K):
        routed = routed + w[:, k:k + 1] * picked(k)
    x = x2_ref[...] + modf_ref[:, 2 * D_MODEL:3 * D_MODEL] * routed
    ms = jnp.mean(x * x, axis=-1, keepdims=True)
    o_ref[...] = x * lax.rsqrt(ms + RMS_EPS) * gfin_ref[...]


def _final(x2, modf, wts, yg, g_final, tok_block_offset, tm):
    B, S, _ = x2.shape
    nt = S // tm
    tok = lambda b, i: (b, i, 0)
    mod_rows = modf.shape[1]
    mod_spec = (pl.BlockSpec((None, 1, 3 * D_MODEL), lambda b, i: (b, 0, 0)) if mod_rows == 1
                else pl.BlockSpec((None, tm, 3 * D_MODEL), tok))
    return pl.pallas_call(
        _final_kernel,
        grid=(B, nt),
        in_specs=[pl.BlockSpec((None, tm, D_MODEL), tok), mod_spec,
                  pl.BlockSpec((None, tm, LANES), tok),
                  pl.BlockSpec((TOP_K, ROW_CHUNKS, tm, CHUNK),
                               lambda b, i: (0, 0, tok_block_offset + b * nt + i, 0)),
                  _const_spec((1, D_MODEL))],
        out_specs=pl.BlockSpec((None, tm, D_MODEL), tok),
        out_shape=jax.ShapeDtypeStruct((B, S, D_MODEL), F32),
        compiler_params=_params(("arbitrary", "arbitrary")),
        name="final",
    )(x2, modf, wts, yg, g_final)


def _sc_worker_id():
    return lax.axis_index("s") * 2 + lax.axis_index("c")


def _dispatch_rows(h_all, dest_blocks, rows_out):
    nsteps = dest_blocks.shape[0]
    mesh = plsc.VectorSubcoreMesh(core_axis_name="c", subcore_axis_name="s")

    @functools.partial(
        pl.kernel, mesh=mesh,
        out_type=jax.ShapeDtypeStruct((ROW_CHUNKS, rows_out, CHUNK), F32),
        scratch_types=[pltpu.VMEM((TOP_K, SC_WINDOW), I32),
                       pltpu.VMEM((SC_WINDOW, CHUNK), F32),
                       pltpu.SemaphoreType.DMA],
    )
    def k(x_hbm, d_hbm, o_hbm, idx_v, rows_v, sem):
        @pl.loop(_sc_worker_id(), nsteps, step=SC_WORKERS)
        def _(s):
            base = pl.multiple_of(s * SC_WINDOW, SC_WINDOW)
            pltpu.sync_copy(d_hbm.at[s], idx_v)
            for c in range(ROW_CHUNKS):
                pltpu.sync_copy(x_hbm.at[c, pl.ds(base, SC_WINDOW)], rows_v)
                for kk in range(TOP_K):
                    pltpu.async_copy(rows_v, o_hbm.at[c].at[idx_v.at[kk]], sem).wait()

    return k(h_all, dest_blocks)


def _gather_rows(y_sorted, dest_blocks):
    nsteps = dest_blocks.shape[0]
    ntok = nsteps * SC_WINDOW
    mesh = plsc.VectorSubcoreMesh(core_axis_name="c", subcore_axis_name="s")

    @functools.partial(
        pl.kernel, mesh=mesh,
        out_type=jax.ShapeDtypeStruct((TOP_K, ROW_CHUNKS, ntok, CHUNK), F32),
        scratch_types=[pltpu.VMEM((TOP_K, SC_WINDOW), I32),
                       pltpu.VMEM((SC_WINDOW, CHUNK), F32),
                       pltpu.SemaphoreType.DMA],
    )
    def k(y_hbm, d_hbm, o_hbm, idx_v, rows_v, sem):
        @pl.loop(_sc_worker_id(), nsteps, step=SC_WORKERS)
        def _(s):
            base = pl.multiple_of(s * SC_WINDOW, SC_WINDOW)
            pltpu.sync_copy(d_hbm.at[s], idx_v)
            for c in range(ROW_CHUNKS):
                for kk in range(TOP_K):
                    pltpu.async_copy(y_hbm.at[c].at[idx_v.at[kk]], rows_v, sem).wait()
                    pltpu.sync_copy(rows_v, o_hbm.at[kk, c, pl.ds(base, SC_WINDOW)])

    return k(y_sorted, dest_blocks)


PAST_LEN = 8192


def _rope_tables(pos):
    half = HEAD_DIM // 2
    inv_freq = ROPE_THETA ** (-jnp.arange(half, dtype=F32) / half)
    ang = pos.astype(F32)[:, None] * inv_freq[None, :]
    cos, sin = jnp.cos(ang), jnp.sin(ang)
    reps = LANES // HEAD_DIM
    return jnp.tile(cos, (1, 2 * reps)), jnp.tile(jnp.concatenate([-sin, sin], axis=1), (1, reps))


def _stream_order(tab, tm, d):
    s = tab.shape[0]
    return tab.reshape(s // tm, tm // d, d, LANES).transpose(0, 2, 1, 3).reshape(s, LANES)


def _kv_tail(kg, vg, keep):
    b, d, l, _ = kg.shape
    n = keep // d

    def natural(t):
        t = t[:, :, l - n:, :].transpose(0, 2, 1, 3)
        return t.reshape(b, keep, HEADS, HEAD_DIM)

    return jnp.stack([natural(kg), natural(vg)], axis=2).astype(F32)[None]


def kernel(x_prompt, x_sample, cache_kv_w128, cache_kv_w512, cache_kv_w2048, state_conv, c_prompt, c_sample,
           g_mix, w_ada_mix, b_ada_mix, w_in, w_dw, b_dw, ln_conv_g, ln_conv_b, w_conv_out, w_att_out, w_o,
           g_ffn, w_ada_ffn, b_ada_ffn, w_router, b_router, w_exp_gate, w_exp_up, w_exp_down,
           w_sh_gate, w_sh_up, w_sh_down, g_final):
    B, S, _ = x_prompt.shape
    ns, T, _ = x_sample.shape
    assert g_mix.shape[0] == 1 and T == 1
    tm = min(TOKEN_TILE, S)
    span = DILATIONS[2] * 16
    assert S % tm == 0 and tm % span == 0 and S % (DILATIONS[2] * Q_BLOCK) == 0
    assert ns % SC_WINDOW == 0 and (B * S) % ns == 0
    caches = (cache_kv_w128, cache_kv_w512, cache_kv_w2048)

    row = lambda v: v.reshape(1, -1)
    w_in_b = w_in[0].astype(BF16)
    w_co_b = w_conv_out[0].astype(BF16)
    w_ao_b = w_att_out[0].astype(BF16)
    w_o_b = w_o[0].astype(BF16)
    wsg_b = w_sh_gate[0].astype(BF16)
    wsu_b = w_sh_up[0].astype(BF16)
    wsd_b = w_sh_down[0].astype(BF16)

    n_c = B + ns
    c_all = jnp.concatenate([c_prompt, c_sample], axis=0)
    c_all = jnp.pad(c_all, ((0, -n_c % 8), (0, 0)))
    mod_mix = _ada(c_all, w_ada_mix[0], b_ada_mix[0])
    mod_ffn = _ada(c_all, w_ada_ffn[0], b_ada_ffn[0])
    modm_p = mod_mix[:B].reshape(B, 1, 3 * D_MODEL)
    modf_p = mod_ffn[:B].reshape(B, 1, 3 * D_MODEL)
    modm_s = mod_mix[B:n_c].reshape(1, ns, 3 * D_MODEL)
    modf_s = mod_ffn[B:n_c].reshape(1, ns, 3 * D_MODEL)

    cos_p, sin_p = _rope_tables(jnp.arange(S, dtype=I32))
    tables = []
    for d in DILATIONS:
        tables += [_stream_order(cos_p, tm, d), _stream_order(sin_p, tm, d)]
    (q0, k0, v0, q1, k1, v1, q2, k2, v2, gc_p, ga_p, utail) = _inproj_prompt(
        x_prompt, modm_p, row(g_mix[0]), w_in_b, tables, w_dw[0], row(b_dw[0]),
        row(ln_conv_g[0]), row(ln_conv_b[0]), w_co_b, tm)
    conv_prompt = utail[:, CONV_HALO - (CONV_WIDTH - 1):][None]

    attn_in = []
    lse_in = []
    kv_prompt = []
    for (qg, kg, vg), d in zip(((q0, k0, v0), (q1, k1, v1), (q2, k2, v2)), DILATIONS):
        l = S // d
        flat = lambda t: t.reshape(B * d, l, D_SLOT)
        o, lse = _attn_prompt(flat(qg), flat(kg), flat(vg))
        if d == 1:
            attn_in.append(o.reshape(B, S, D_SLOT))
            lse_in.append(lse.reshape(B, S, LANES))
        else:
            attn_in.append(o.reshape(B, d, l, D_SLOT))
            lse_in.append(lse.reshape(B, d, l, LANES))
        keep = min(Q_BLOCK * d, S)
        kv_prompt.append(_kv_tail(kg.reshape(B, d, l, D_SLOT), vg.reshape(B, d, l, D_SLOT), keep))

    zero_carry = jnp.zeros((1, N_EXPERTS), F32)
    h2_p, x2_p, idx_p, wts_p, pos_p, cnt_p = _post(
        x_prompt, modm_p, (*attn_in, *lse_in), gc_p, ga_p, w_ao_b, w_o_b, row(g_ffn[0]), modf_p,
        w_router[0], row(b_router[0]), wsg_b, wsu_b, wsd_b, zero_carry, tm, True)

    cos_s, sin_s = _rope_tables(jnp.full((1,), PAST_LEN, I32))
    u_s, q_s, k_s, v_s, sgc_s, ga_s = _inproj_sample(
        x_sample.reshape(ns, D_MODEL), mod_mix[B:n_c], row(g_mix[0]), w_in_b, cos_s, sin_s)
    col = lambda t: t.reshape(ns, N_GROUPS, HEADS, HEAD_DIM, 1)
    q_c, k_c, v_c = col(q_s), col(k_s), col(v_s)
    kv_sample, outs_s, lses_s = [], [], []
    for gi, (cache, d) in enumerate(zip(caches, DILATIONS)):
        cache_t = jnp.transpose(cache[0], (0, 2, 3, 4, 1))
        new_t, o, lse = _sample_cache(cache_t, q_c, k_c, v_c, gi, d)
        kv_sample.append(jnp.transpose(new_t, (0, 4, 1, 2, 3))[None])
        outs_s.append(o.reshape(ns, D_SLOT))
        lses_s.append(lse.reshape(ns, D_SLOT))
    state_t = jnp.transpose(state_conv[0], (1, 0, 2))
    gc_s, att_s, cst_t = _sample_mix(state_t, u_s, sgc_s, w_dw[0], row(b_dw[0]), row(ln_conv_g[0]),
                                     row(ln_conv_b[0]), w_co_b, outs_s, lses_s)
    conv_sample = jnp.transpose(cst_t, (1, 0, 2))[None]
    as3 = lambda t: t.reshape(1, ns, t.shape[-1])
    h2_s, x2_s, idx_s, wts_s, pos_s, cnt = _post(
        as3(x_sample.reshape(ns, D_MODEL)), modm_s, (as3(att_s),), as3(gc_s), as3(ga_s), w_ao_b, w_o_b,
        row(g_ffn[0]), modf_s, w_router[0], row(b_router[0]), wsg_b, wsu_b, wsd_b, cnt_p, ns, False)

    bm = EXPERT_BLOCK
    ntok = B * S + ns
    nblk = (ntok * TOP_K + N_EXPERTS * (bm - 1)) // bm
    counts = cnt[0].astype(I32)
    padded = (counts + bm - 1) // bm * bm
    pend = jnp.cumsum(padded)
    pstart = pend - padded
    first8 = lambda t: t.reshape(-1, LANES)[:, :TOP_K]
    idx_all = jnp.concatenate([first8(idx_p), first8(idx_s)], axis=0)
    pos_all = jnp.concatenate([first8(pos_p), first8(pos_s)], axis=0)
    hit = idx_all[:, :, None] == jnp.arange(N_EXPERTS, dtype=I32)[None, None, :]
    dest = jnp.sum(jnp.where(hit, pstart[None, None, :], 0), axis=-1) + pos_all
    dest_blocks = dest.reshape(ntok // SC_WINDOW, SC_WINDOW, TOP_K).transpose(0, 2, 1)
    blk_row0 = jnp.arange(nblk, dtype=I32) * bm
    blk_e = jnp.minimum(jnp.searchsorted(pend, blk_row0, side='right'), N_EXPERTS - 1).astype(I32)
    blk_valid = jnp.clip(counts[blk_e] - (blk_row0 - pstart[blk_e]), 0, bm).astype(I32)

    h_all = jnp.concatenate([h2_p, h2_s], axis=1)
    x_sorted = _dispatch_rows(h_all, dest_blocks, nblk * bm)
    y_sorted = _experts(blk_e, blk_valid, x_sorted, w_exp_gate[0], w_exp_up[0], w_exp_down[0], bm)
    yg = _gather_rows(y_sorted, dest_blocks)

    y_prompt = _final(x2_p, modf_p, wts_p, yg, row(g_final), 0, tm)
    y_sample = _final(x2_s, modf_s, wts_s, yg, row(g_final), (B * S) // ns, ns).reshape(ns, 1, D_MODEL)

    return (y_prompt, y_sample, kv_prompt[0], kv_prompt[1], kv_prompt[2], conv_prompt,
            kv_sample[0], kv_sample[1], kv_sample[2], conv_sample)
```

```python
import functools

import jax
import jax.numpy as jnp
from jax import lax
from jax.experimental import pallas as pl
from jax.experimental.pallas import tpu as pltpu
from jax.experimental.pallas import tpu_sc as plsc

F32 = jnp.float32
BF16 = jnp.bfloat16
I32 = jnp.int32

D_MODEL = 1024
D_CONV = 1024
CONV_WIDTH = 31
HEAD_DIM = 64
HEADS = 8
D_SLOT = HEADS * HEAD_DIM
DILATIONS = (1, 4, 16)
N_GROUPS = 3
D_ATT = N_GROUPS * D_SLOT
Q_BLOCK = 128
ROPE_THETA = 10000.0
N_EXPERTS = 256
TOP_K = 8
N_EXPERT_GROUPS = 8
TOPK_GROUPS = 4
GROUP_SIZE = N_EXPERTS // N_EXPERT_GROUPS
D_EXPERT = 256
ROUTED_SCALE = 2.5
RMS_EPS = 1e-6
LN_EPS = 1e-5

COL_A = 0
COL_B = D_CONV
COL_Q = 2 * D_CONV
COL_K = COL_Q + D_ATT
COL_V = COL_K + D_ATT
COL_GC = COL_V + D_ATT
COL_GA = COL_GC + D_MODEL
D_IN = COL_GA + D_MODEL

LANES = 128
CONV_HALO = 32
TOKEN_TILE = 512
EXPERT_BLOCK = 256
EXPERT_BLOCK_FEW = 64
ROW_CHUNKS = 2
CHUNK = D_MODEL // 2 // ROW_CHUNKS
SC_WORKERS = 32
SC_WINDOW = 128
NEG_BIG = -1e30
VMEM_LIMIT = 56 * 1024 * 1024


def _sigmoid(x):
    return 1.0 / (1.0 + jnp.exp(-x))


def _const_spec(shape):
    nd = len(shape)
    return pl.BlockSpec(shape, lambda *_: (0,) * nd, pipeline_mode=pl.Buffered(1))


def _params(sem):
    return pltpu.CompilerParams(dimension_semantics=sem, vmem_limit_bytes=VMEM_LIMIT)


def _ada_kernel(c_ref, w_ref, b_ref, o_ref):
    c = c_ref[...]
    s = (c * _sigmoid(c)).astype(BF16)
    o_ref[...] = jnp.dot(s, w_ref[...].astype(BF16), preferred_element_type=F32) + b_ref[...]


def _ada(c_all, w, b):
    rows = c_all.shape[0]
    cols = w.shape[1]
    tn = 768
    return pl.pallas_call(
        _ada_kernel,
        grid=(cols // tn,),
        in_specs=[pl.BlockSpec((rows, D_MODEL), lambda j: (0, 0)),
                  pl.BlockSpec((D_MODEL, tn), lambda j: (0, j)),
                  pl.BlockSpec((1, tn), lambda j: (0, j))],
        out_specs=pl.BlockSpec((rows, tn), lambda j: (0, j)),
        out_shape=jax.ShapeDtypeStruct((rows, cols), F32),
        compiler_params=_params(("arbitrary",)),
        name="ada",
    )(c_all, w, b.reshape(1, cols))


def _modulated_norm(x, g, mod):
    shift = mod[:, 0:D_MODEL]
    scale = mod[:, D_MODEL:2 * D_MODEL]
    ms = jnp.mean(x * x, axis=-1, keepdims=True)
    return (x * lax.rsqrt(ms + RMS_EPS)) * g * (1.0 + scale) + shift


def _rope(t, cos, sin_signed):
    lane = lax.broadcasted_iota(I32, (t.shape[0], LANES), 1)
    first_half = (lane & (HEAD_DIM - 1)) < (HEAD_DIM // 2)
    outs = []
    for j in range(D_SLOT // LANES):
        ch = t[:, j * LANES:(j + 1) * LANES]
        rot = jnp.where(first_half, pltpu.roll(ch, LANES - HEAD_DIM // 2, 1), pltpu.roll(ch, HEAD_DIM // 2, 1))
        outs.append(ch * cos + rot * sin_signed)
    return jnp.concatenate(outs, axis=1)


def _layer_norm_swish(y, g, b):
    mu = jnp.mean(y, axis=-1, keepdims=True)
    yc = y - mu
    var = jnp.mean(yc * yc, axis=-1, keepdims=True)
    z = yc * lax.rsqrt(var + LN_EPS) * g + b
    return z * _sigmoid(z)


def _inproj_kernel(x_ref, mod_ref, g_ref, w_ref, c0_ref, s0_ref, c1_ref, s1_ref, c2_ref, s2_ref,
                   wdw_ref, bdw_ref, lng_ref, lnb_ref, wco_ref,
                   q0_ref, k0_ref, v0_ref, q1_ref, k1_ref, v1_ref, q2_ref, k2_ref, v2_ref,
                   gc_ref, ga_ref, ut_ref,
                   hf_ref, hb0_ref, hb1_ref, hb2_ref, ubuf_ref, *, tm):
    i = pl.program_id(1)
    cw = 512

    h = _modulated_norm(x_ref[...], g_ref[...], mod_ref[...])
    nlc = D_MODEL // LANES
    for c in range(nlc):
        hf_ref[c] = h[:, c * LANES:(c + 1) * LANES]
    hb0_ref[...] = h.astype(BF16)
    for d, hb in ((DILATIONS[1], hb1_ref), (DILATIONS[2], hb2_ref)):
        n = tm // d
        for r in range(d):
            for c in range(nlc):
                hb[r * n:(r + 1) * n, c * LANES:(c + 1) * LANES] = hf_ref[c, pl.ds(r, n, stride=d), :].astype(BF16)

    def mm(hb, col, width=cw):
        return jnp.dot(hb[...], w_ref[:, col:col + width], preferred_element_type=F32)

    @pl.when(i == 0)
    def _():
        ubuf_ref[:, 0:CONV_HALO, :] = jnp.zeros((nlc, CONV_HALO, LANES), F32)

    for c in range(0, D_CONV, cw):
        a = mm(hb0_ref, COL_A + c)
        b = mm(hb0_ref, COL_B + c)
        u = a * _sigmoid(b)
        for cc in range(cw // LANES):
            ubuf_ref[c // LANES + cc, CONV_HALO:CONV_HALO + tm, :] = u[:, cc * LANES:(cc + 1) * LANES]

    groups = ((hb0_ref, 1, c0_ref, s0_ref, q0_ref, k0_ref, v0_ref),
              (hb1_ref, DILATIONS[1], c1_ref, s1_ref, q1_ref, k1_ref, v1_ref),
              (hb2_ref, DILATIONS[2], c2_ref, s2_ref, q2_ref, k2_ref, v2_ref))
    for gi, (hb, d, c_ref, s_ref, q_ref, k_ref, v_ref) in enumerate(groups):
        cos = c_ref[...]
        sin = s_ref[...]
        n = tm // d

        def put(ref, val):
            vb = val.astype(BF16)
            if d == 1:
                ref[...] = vb
            else:
                for r in range(d):
                    ref[r] = vb[r * n:(r + 1) * n, :]

        put(q_ref, _rope(mm(hb, COL_Q + gi * D_SLOT), cos, sin) * (HEAD_DIM ** -0.5))
        put(k_ref, _rope(mm(hb, COL_K + gi * D_SLOT), cos, sin))
        put(v_ref, mm(hb, COL_V + gi * D_SLOT))

    for c in range(0, D_MODEL, cw):
        ga_ref[:, c:c + cw] = _sigmoid(mm(hb0_ref, COL_GA + c)).astype(BF16)

    rb = 32
    first_tap = CONV_HALO - (CONV_WIDTH - 1)

    def conv_rows(r, carry):
        r0 = pl.multiple_of(r * rb, rb)
        for c in range(nlc):
            cs = slice(c * LANES, (c + 1) * LANES)
            acc = jnp.broadcast_to(bdw_ref[:, cs], (rb, LANES))
            for k in range(CONV_WIDTH):
                acc = acc + wdw_ref[k:k + 1, cs] * ubuf_ref[c, pl.ds(r0 + (first_tap + k), rb), :]
            hf_ref[c, pl.ds(r0, rb), :] = acc
        return carry

    lax.fori_loop(0, tm // rb, conv_rows, 0)

    y_dw = jnp.concatenate([hf_ref[c] for c in range(nlc)], axis=1)
    z = _layer_norm_swish(y_dw, lng_ref[...], lnb_ref[...]).astype(BF16)
    for c in range(0, D_MODEL, cw):
        co = jnp.dot(z, wco_ref[:, c:c + cw], preferred_element_type=F32)
        gc_ref[:, c:c + cw] = (_sigmoid(mm(hb0_ref, COL_GC + c)) * co).astype(BF16)

    for c in range(nlc):
        tail = ubuf_ref[c, tm:tm + CONV_HALO, :]
        ut_ref[:, c * LANES:(c + 1) * LANES] = tail
        ubuf_ref[c, 0:CONV_HALO, :] = tail


def _inproj_prompt(x, mod, g, w_in_b, tables, w_dw, b_dw, ln_g, ln_b, w_co_b, tm):
    B, S, _ = x.shape
    nt = S // tm
    d1, d2 = DILATIONS[1], DILATIONS[2]
    tok = lambda b, i: (b, i, 0)
    tab = pl.BlockSpec((tm, LANES), lambda b, i: (i, 0))
    in_specs = [
        pl.BlockSpec((None, tm, D_MODEL), tok),
        pl.BlockSpec((None, 1, 3 * D_MODEL), lambda b, i: (b, 0, 0)),
        _const_spec((1, D_MODEL)),
        _const_spec((D_MODEL, D_IN)),
        tab, tab, tab, tab, tab, tab,
        _const_spec((CONV_WIDTH, D_CONV)),
        _const_spec((1, D_CONV)),
        _const_spec((1, D_CONV)),
        _const_spec((1, D_CONV)),
        _const_spec((D_CONV, D_MODEL)),
    ]
    nat = pl.BlockSpec((None, tm, D_SLOT), tok)
    st1 = pl.BlockSpec((None, d1, tm // d1, D_SLOT), lambda b, i: (b, 0, i, 0))
    st2 = pl.BlockSpec((None, d2, tm // d2, D_SLOT), lambda b, i: (b, 0, i, 0))
    out_specs = [nat, nat, nat, st1, st1, st1, st2, st2, st2,
                 pl.BlockSpec((None, tm, D_MODEL), tok),
                 pl.BlockSpec((None, tm, D_MODEL), tok),
                 pl.BlockSpec((None, CONV_HALO, D_CONV), lambda b, i: (b, 0, 0))]
    s0 = jax.ShapeDtypeStruct((B, S, D_SLOT), BF16)
    s1 = jax.ShapeDtypeStruct((B, d1, S // d1, D_SLOT), BF16)
    s2 = jax.ShapeDtypeStruct((B, d2, S // d2, D_SLOT), BF16)
    out_shape = [s0, s0, s0, s1, s1, s1, s2, s2, s2,
                 jax.ShapeDtypeStruct((B, S, D_MODEL), BF16),
                 jax.ShapeDtypeStruct((B, S, D_MODEL), BF16),
                 jax.ShapeDtypeStruct((B, CONV_HALO, D_CONV), F32)]
    scratch = [pltpu.VMEM((D_MODEL // LANES, tm, LANES), F32),
               pltpu.VMEM((tm, D_MODEL), BF16),
               pltpu.VMEM((tm, D_MODEL), BF16),
               pltpu.VMEM((tm, D_MODEL), BF16),
               pltpu.VMEM((D_CONV // LANES, tm + CONV_HALO, LANES), F32)]
    return pl.pallas_call(
        functools.partial(_inproj_kernel, tm=tm),
        grid=(B, nt),
        in_specs=in_specs, out_specs=out_specs, out_shape=out_shape,
        scratch_shapes=scratch,
        compiler_params=_params(("arbitrary", "arbitrary")),
        name="inproj",
    )(x, mod, g, w_in_b, *tables, w_dw, b_dw, ln_g, ln_b, w_co_b)


def _attn_kernel(q_ref, kp_ref, kc_ref, vp_ref, vc_ref, o_ref, lse_ref, k_all, v_all):
    j = pl.program_id(1)
    k_all[0:Q_BLOCK, :] = kp_ref[...]
    k_all[Q_BLOCK:2 * Q_BLOCK, :] = kc_ref[...]
    v_all[0:Q_BLOCK, :] = vp_ref[...]
    v_all[Q_BLOCK:2 * Q_BLOCK, :] = vc_ref[...]
    qi = lax.broadcasted_iota(I32, (Q_BLOCK, 2 * Q_BLOCK), 0)
    ki = lax.broadcasted_iota(I32, (Q_BLOCK, 2 * Q_BLOCK), 1)
    back = Q_BLOCK + qi - ki
    mask = jnp.logical_and(jnp.logical_and(back >= 0, back <= Q_BLOCK),
                           jnp.logical_or(ki >= Q_BLOCK, j > 0))
    lane = lax.broadcasted_iota(I32, (Q_BLOCK, LANES), 1)
    even = lax.broadcasted_iota(I32, (1, LANES), 1) < HEAD_DIM
    zero = jnp.zeros((), BF16)
    nt = (((1,), (1,)), ((), ()))
    scores = []
    for h in range(HEADS):
        pair = slice((h // 2) * LANES, (h // 2 + 1) * LANES)
        mine = even if h % 2 == 0 else jnp.logical_not(even)
        qh = jnp.where(mine, q_ref[:, pair], zero)
        scores.append(lax.dot_general(qh, k_all[:, pair], nt, preferred_element_type=F32))
    probs, dens = [], []
    lse_all = jnp.zeros((Q_BLOCK, LANES), F32)
    for h in range(HEADS):
        s = jnp.where(mask, scores[h], NEG_BIG)
        m = jnp.max(s, axis=-1, keepdims=True)
        p = jnp.exp(s - m)
        den = jnp.sum(p, axis=-1, keepdims=True)
        probs.append(p.astype(BF16))
        dens.append(den)
        lse_all = jnp.where(lane == h, m + jnp.log(den), lse_all)
    for hp in range(HEADS // 2):
        pair = slice(hp * LANES, (hp + 1) * LANES)
        v2 = v_all[:, pair]
        oe = jnp.dot(probs[2 * hp], v2, preferred_element_type=F32) / dens[2 * hp]
        oo = jnp.dot(probs[2 * hp + 1], v2, preferred_element_type=F32) / dens[2 * hp + 1]
        o_ref[:, pair] = jnp.where(even, oe, oo).astype(BF16)
    lse_ref[...] = lse_all


def _attn_prompt(q, k, v):
    ns, L, _ = q.shape
    nb = L // Q_BLOCK
    cur = pl.BlockSpec((None, Q_BLOCK, D_SLOT), lambda n, j: (n, j, 0))
    prev = pl.BlockSpec((None, Q_BLOCK, D_SLOT), lambda n, j: (n, jnp.maximum(j - 1, 0), 0))
    return pl.pallas_call(
        _attn_kernel,
        grid=(ns, nb),
        in_specs=[cur, prev, cur, prev, cur],
        out_specs=[cur, pl.BlockSpec((None, Q_BLOCK, LANES), lambda n, j: (n, j, 0))],
        out_shape=[jax.ShapeDtypeStruct((ns, L, D_SLOT), BF16),
                   jax.ShapeDtypeStruct((ns, L, LANES), F32)],
        scratch_shapes=[pltpu.VMEM((2 * Q_BLOCK, D_SLOT), BF16), pltpu.VMEM((2 * Q_BLOCK, D_SLOT), BF16)],
        compiler_params=_params(("arbitrary", "arbitrary")),
        name="attn",
    )(q, k, k, v, v)


def _route_t(logits_t, b_col, carry, tm):
    ninf = -jnp.inf
    scores = _sigmoid(logits_t)
    sel = scores + b_col
    rowf = lax.broadcasted_iota(I32, (N_EXPERTS, tm), 0).astype(F32)
    past_end = float(N_EXPERTS)

    def first_max(x, rows):
        m = jnp.max(x, axis=0, keepdims=True)
        return m, jnp.min(jnp.where(x == m, rows, past_end), axis=0, keepdims=True)

    gs = []
    rowg = lax.broadcasted_iota(I32, (GROUP_SIZE, tm), 0).astype(F32)
    for g in range(N_EXPERT_GROUPS):
        rs = slice(g * GROUP_SIZE, (g + 1) * GROUP_SIZE)
        m1, i1 = first_max(sel[rs], rowg)
        m2 = jnp.max(jnp.where(rowg == i1, ninf, sel[rs]), axis=0, keepdims=True)
        gs.append(m1 + m2)
    pieces = []
    for g in range(N_EXPERT_GROUPS):
        beaten = jnp.zeros((1, tm), F32)
        for g2 in range(N_EXPERT_GROUPS):
            if g2 != g:
                better = gs[g2] >= gs[g] if g2 < g else gs[g2] > gs[g]
                beaten = beaten + jnp.where(better, 1.0, 0.0)
        rs = slice(g * GROUP_SIZE, (g + 1) * GROUP_SIZE)
        pieces.append(jnp.where(beaten < TOPK_GROUPS, sel[rs], ninf))
    selm = jnp.concatenate(pieces, axis=0)

    idx_rows, w_rows = [], []
    picked = jnp.zeros((N_EXPERTS, tm), F32)
    for _ in range(TOP_K):
        _, ik = first_max(selm, rowf)
        hit = rowf == ik
        w_rows.append(jnp.sum(jnp.where(hit, scores, 0.0), axis=0, keepdims=True))
        selm = jnp.where(hit, ninf, selm)
        picked = jnp.where(hit, 1.0, picked)
        idx_rows.append(ik)
    wsum = w_rows[0]
    for wk in w_rows[1:]:
        wsum = wsum + wk
    denom = wsum + 1e-20

    ti = lax.broadcasted_iota(I32, (tm, tm), 0)
    tj = lax.broadcasted_iota(I32, (tm, tm), 1)
    earlier = jnp.where(ti < tj, 1.0, 0.0).astype(BF16)
    before = jnp.dot(picked.astype(BF16), earlier, preferred_element_type=F32) + carry
    new_carry = carry + jnp.sum(picked, axis=1, keepdims=True)
    pos_rows = [jnp.sum(jnp.where(rowf == ik, before, 0.0), axis=0, keepdims=True) for ik in idx_rows]

    idx_o = jnp.concatenate(idx_rows, axis=0).astype(I32)
    w_o = jnp.concatenate([wk / denom * ROUTED_SCALE for wk in w_rows], axis=0)
    pos_o = jnp.concatenate(pos_rows, axis=0).astype(I32)
    return idx_o, w_o, pos_o, new_carry


def _pack_rows(xb):
    bits = lax.bitcast_convert_type(xb.astype(F32), jnp.uint32)
    half = D_MODEL // 2
    word = bits[:, half:] | (bits[:, :half] >> 16)
    return lax.bitcast_convert_type(word, F32)


def _unpack_rows(words):
    bits = lax.bitcast_convert_type(words, jnp.uint32)
    lo = lax.bitcast_convert_type(bits << 16, F32)
    hi = lax.bitcast_convert_type(bits & jnp.uint32(0xFFFF0000), F32)
    return jnp.concatenate([lo, hi], axis=1)


def _post_kernel(*refs, tm, combine):
    refs = list(refs)
    x_ref, modm_ref = refs[0:2]
    p = 2
    if combine:
        o0_ref, o1_ref, o2_ref, l0_ref, l1_ref, l2_ref = refs[p:p + 6]
        p += 6
    else:
        att_ref = refs[p]
        p += 1
    (gc_ref, ga_ref, wao_ref, wo_ref, gffn_ref, modf_ref, wrh_ref, wrl_ref, br_ref,
     wsg_ref, wsu_ref, wsd_ref, cin_ref) = refs[p:p + 13]
    p += 13
    h2_ref, x2_ref, idx_ref, wts_ref, pos_ref, cnt_ref = refs[p:p + 6]
    p += 6
    if combine:
        o1n_ref, o2n_ref, l1n_ref, l2n_ref = refs[p:p + 4]

    first = jnp.logical_and(pl.program_id(0) == 0, pl.program_id(1) == 0)

    @pl.when(first)
    def _():
        cnt_ref[...] = cin_ref[...]

    if combine:
        nsc = D_SLOT // LANES
        for d, o_ref, l_ref, on_ref, ln_ref in ((DILATIONS[1], o1_ref, l1_ref, o1n_ref, l1n_ref),
                                                (DILATIONS[2], o2_ref, l2_ref, o2n_ref, l2n_ref)):
            n = tm // d
            for r in range(d):
                ln_ref[pl.ds(r, n, stride=d), :] = l_ref[r]
                orow = o_ref[r].astype(F32)
                for c in range(nsc):
                    on_ref[c, pl.ds(r, n, stride=d), :] = orow[:, c * LANES:(c + 1) * LANES]
        l0 = l0_ref[...]
        l1 = l1n_ref[...]
        l2 = l2n_ref[...]
        mx = jnp.maximum(jnp.maximum(l0, l1), l2)
        e0 = jnp.exp(l0 - mx)
        e1 = jnp.exp(l1 - mx)
        e2 = jnp.exp(l2 - mx)
        esum = e0 + e1 + e2
        even = lax.broadcasted_iota(I32, (tm, LANES), 1) < HEAD_DIM

        def expand(w):
            wide = lambda h: jnp.broadcast_to(w[:, h:h + 1], (tm, LANES))
            return jnp.concatenate([jnp.where(even, wide(2 * c), wide(2 * c + 1)) for c in range(nsc)], axis=1)

        o1n = jnp.concatenate([o1n_ref[c] for c in range(nsc)], axis=1)
        o2n = jnp.concatenate([o2n_ref[c] for c in range(nsc)], axis=1)
        att = (expand(e0 / esum) * o0_ref[...].astype(F32) + expand(e1 / esum) * o1n
               + expand(e2 / esum) * o2n)
    else:
        att = att_ref[...]

    att_out = jnp.dot(att.astype(BF16), wao_ref[...], preferred_element_type=F32)
    merged = gc_ref[...].astype(F32) + ga_ref[...].astype(F32) * att_out
    y = jnp.dot(merged.astype(BF16), wo_ref[...], preferred_element_type=F32)
    x1 = x_ref[...] + modm_ref[:, 2 * D_MODEL:3 * D_MODEL] * y

    modf = modf_ref[...]
    h2 = _modulated_norm(x1, gffn_ref[...], modf)
    h2b = h2.astype(BF16)
    words = _pack_rows(h2b)
    for c in range(ROW_CHUNKS):
        h2_ref[c] = words[:, c * CHUNK:(c + 1) * CHUNK]
    sg = jnp.dot(h2b, wsg_ref[...], preferred_element_type=F32)
    su = jnp.dot(h2b, wsu_ref[...], preferred_element_type=F32)
    sh = jnp.dot((sg * _sigmoid(sg) * su).astype(BF16), wsd_ref[...], preferred_element_type=F32)
    x2_ref[...] = x1 + modf[:, 2 * D_MODEL:3 * D_MODEL] * sh

    h2l = (h2 - h2b.astype(F32)).astype(BF16)
    nt = (((1,), (1,)), ((), ()))
    logits_t = (lax.dot_general(wrh_ref[...], h2b, nt, preferred_element_type=F32)
                + lax.dot_general(wrl_ref[...], h2b, nt, preferred_element_type=F32)
                + lax.dot_general(wrh_ref[...], h2l, nt, preferred_element_type=F32))
    idx_o, w_o, pos_o, new_carry = _route_t(logits_t, br_ref[...], cnt_ref[...], tm)
    idx_ref[...] = idx_o
    wts_ref[...] = w_o
    pos_ref[...] = pos_o
    cnt_ref[...] = new_carry


def _post(x, modm, attn_inputs, gc, ga, w_ao_b, w_o_b, g_ffn, modf, wr_hi, wr_lo, b_router,
          wsg_b, wsu_b, wsd_b, carry_in, tm, combine):
    B, S, _ = x.shape
    nt = S // tm
    tok = lambda b, i: (b, i, 0)
    mod_rows = modm.shape[1]
    mod_spec = (pl.BlockSpec((None, 1, 3 * D_MODEL), lambda b, i: (b, 0, 0)) if mod_rows == 1
                else pl.BlockSpec((None, tm, 3 * D_MODEL), tok))
    full = pl.BlockSpec((None, tm, D_MODEL), tok)
    in_specs = [full, mod_spec]
    scratch = []
    if combine:
        d1, d2 = DILATIONS[1], DILATIONS[2]
        in_specs += [pl.BlockSpec((None, tm, D_SLOT), tok),
                     pl.BlockSpec((None, d1, tm // d1, D_SLOT), lambda b, i: (b, 0, i, 0)),
                     pl.BlockSpec((None, d2, tm // d2, D_SLOT), lambda b, i: (b, 0, i, 0)),
                     pl.BlockSpec((None, tm, LANES), tok),
                     pl.BlockSpec((None, d1, tm // d1, LANES), lambda b, i: (b, 0, i, 0)),
                     pl.BlockSpec((None, d2, tm // d2, LANES), lambda b, i: (b, 0, i, 0))]
        scratch = [pltpu.VMEM((D_SLOT // LANES, tm, LANES), F32),
                   pltpu.VMEM((D_SLOT // LANES, tm, LANES), F32),
                   pltpu.VMEM((tm, LANES), F32),
                   pltpu.VMEM((tm, LANES), F32)]
    else:
        in_specs += [pl.BlockSpec((None, tm, D_SLOT), tok)]
    in_specs += [full, full,
                 _const_spec((D_SLOT, D_MODEL)), _const_spec((D_MODEL, D_MODEL)),
                 _const_spec((1, D_MODEL)), mod_spec,
                 _const_spec((N_EXPERTS, D_MODEL)), _const_spec((N_EXPERTS, D_MODEL)),
                 _const_spec((N_EXPERTS, 1)),
                 _const_spec((D_MODEL, D_EXPERT)), _const_spec((D_MODEL, D_EXPERT)),
                 _const_spec((D_EXPERT, D_MODEL)), _const_spec((N_EXPERTS, 1))]
    pack = pl.BlockSpec((TOP_K, tm), lambda b, i: (0, b * nt + i))
    chunked = pl.BlockSpec((ROW_CHUNKS, tm, CHUNK), lambda b, i: (0, b * nt + i, 0))
    out_specs = [chunked, full, pack, pack, pack, pl.BlockSpec((N_EXPERTS, 1), lambda b, i: (0, 0))]
    out_shape = [jax.ShapeDtypeStruct((ROW_CHUNKS, B * S, CHUNK), F32),
                 jax.ShapeDtypeStruct((B, S, D_MODEL), F32),
                 jax.ShapeDtypeStruct((TOP_K, B * S), I32),
                 jax.ShapeDtypeStruct((TOP_K, B * S), F32),
                 jax.ShapeDtypeStruct((TOP_K, B * S), I32),
                 jax.ShapeDtypeStruct((N_EXPERTS, 1), F32)]
    return pl.pallas_call(
        functools.partial(_post_kernel, tm=tm, combine=combine),
        grid=(B, nt),
        in_specs=in_specs, out_specs=out_specs, out_shape=out_shape,
        scratch_shapes=scratch,
        compiler_params=_params(("arbitrary", "arbitrary")),
        name="post_prompt" if combine else "post_sample",
    )(x, modm, *attn_inputs, gc, ga, w_ao_b, w_o_b, g_ffn, modf, wr_hi, wr_lo, b_router,
      wsg_b, wsu_b, wsd_b, carry_in)


def _inproj_sample_kernel(x_ref, mod_ref, g_ref, w_ref, cos_ref, sin_ref,
                          u_ref, q_ref, k_ref, v_ref, sgc_ref, ga_ref):
    cw = 512
    hb = _modulated_norm(x_ref[...], g_ref[...], mod_ref[...]).astype(BF16)

    def mm(col, width=cw):
        return jnp.dot(hb, w_ref[:, col:col + width], preferred_element_type=F32)

    cos = cos_ref[...]
    sin = sin_ref[...]
    for c in range(0, D_CONV, cw):
        u_ref[:, c:c + cw] = mm(COL_A + c) * _sigmoid(mm(COL_B + c))
        sgc_ref[:, c:c + cw] = _sigmoid(mm(COL_GC + c))
        ga_ref[:, c:c + cw] = _sigmoid(mm(COL_GA + c)).astype(BF16)
    for gi in range(N_GROUPS):
        cs = slice(gi * D_SLOT, (gi + 1) * D_SLOT)
        q_ref[:, cs] = _rope(mm(COL_Q + gi * D_SLOT), cos, sin) * (HEAD_DIM ** -0.5)
        k_ref[:, cs] = _rope(mm(COL_K + gi * D_SLOT), cos, sin)
        v_ref[:, cs] = mm(COL_V + gi * D_SLOT)


def _inproj_sample(x, mod, g, w_in_b, cos, sin):
    ns = x.shape[0]
    whole = lambda shape: pl.BlockSpec(shape, lambda i: (0,) * len(shape))
    f = lambda cols, dt=F32: jax.ShapeDtypeStruct((ns, cols), dt)
    return pl.pallas_call(
        _inproj_sample_kernel,
        grid=(1,),
        in_specs=[whole((ns, D_MODEL)), whole((ns, 3 * D_MODEL)), _const_spec((1, D_MODEL)),
                  _const_spec((D_MODEL, D_IN)), whole((1, LANES)), whole((1, LANES))],
        out_specs=[whole((ns, D_CONV)), whole((ns, D_ATT)), whole((ns, D_ATT)), whole((ns, D_ATT)),
                   whole((ns, D_MODEL)), whole((ns, D_MODEL))],
        out_shape=[f(D_CONV), f(D_ATT), f(D_ATT), f(D_ATT), f(D_MODEL), f(D_MODEL, BF16)],
        compiler_params=_params(("arbitrary",)),
        name="inproj_sample",
    )(x, mod, g, w_in_b, cos, sin)


def _sample_mix_kernel(state_ref, u_ref, sgc_ref, wdw_ref, bdw_ref, lng_ref, lnb_ref, wco_ref,
                       o0_ref, o1_ref, o2_ref, l0_ref, l1_ref, l2_ref,
                       gc_ref, att_ref, cst_ref):
    nprev = CONV_WIDTH - 1
    u = u_ref[...]
    y = wdw_ref[nprev:CONV_WIDTH, :] * u + bdw_ref[...]
    for k in range(nprev):
        y = y + wdw_ref[k:k + 1, :] * state_ref[k]
    z = _layer_norm_swish(y, lng_ref[...], lnb_ref[...]).astype(BF16)
    co = jnp.dot(z, wco_ref[...], preferred_element_type=F32)
    gc_ref[...] = (sgc_ref[...] * co).astype(BF16)
    for k in range(nprev - 1):
        cst_ref[k] = state_ref[k + 1]
    cst_ref[nprev - 1] = u

    l0 = l0_ref[...]
    l1 = l1_ref[...]
    l2 = l2_ref[...]
    mx = jnp.maximum(jnp.maximum(l0, l1), l2)
    e0 = jnp.exp(l0 - mx)
    e1 = jnp.exp(l1 - mx)
    e2 = jnp.exp(l2 - mx)
    esum = e0 + e1 + e2
    att_ref[...] = (e0 / esum) * o0_ref[...] + (e1 / esum) * o1_ref[...] + (e2 / esum) * o2_ref[...]


def _sample_mix(state, u, sgc, w_dw, b_dw, ln_g, ln_b, w_co_b, outs, lses):
    ns = state.shape[1]
    nprev = CONV_WIDTH - 1
    whole = lambda *shape: pl.BlockSpec(shape, lambda i: (0,) * len(shape))
    slot = whole(ns, D_SLOT)
    return pl.pallas_call(
        _sample_mix_kernel,
        grid=(1,),
        in_specs=[whole(nprev, ns, D_CONV), whole(ns, D_CONV), whole(ns, D_MODEL),
                  whole(CONV_WIDTH, D_CONV), whole(1, D_CONV), whole(1, D_CONV), whole(1, D_CONV),
                  whole(D_CONV, D_MODEL), slot, slot, slot, slot, slot, slot],
        out_specs=[whole(ns, D_MODEL), slot, whole(nprev, ns, D_CONV)],
        out_shape=[jax.ShapeDtypeStruct((ns, D_MODEL), BF16),
                   jax.ShapeDtypeStruct((ns, D_SLOT), F32),
                   jax.ShapeDtypeStruct((nprev, ns, D_CONV), F32)],
        compiler_params=_params(("arbitrary",)),
        name="sample_mix",
    )(state, u, sgc, w_dw, b_dw, ln_g, ln_b, w_co_b, *outs, *lses)


def _sample_cache_kernel(c_ref, q_ref, kn_ref, vn_ref, co_ref, o_ref, l_ref, *, hb, n, dil):
    lane = lax.broadcasted_iota(I32, (1, n), 1)
    back = n - lane
    use = jnp.logical_and(jnp.logical_and(back % dil == 0, back <= Q_BLOCK * dil), back >= dil)
    last = lax.broadcasted_iota(I32, (HEAD_DIM, n), 1) == n - 1
    rows, news = [], []
    for h in range(hb):
        q = q_ref[h]
        rows.append(jnp.sum(c_ref[0, h] * q, axis=0, keepdims=True))
        news.append(jnp.sum(kn_ref[h] * q, axis=0, keepdims=True))
    s = jnp.where(use, jnp.concatenate(rows, axis=0), NEG_BIG)
    sn = jnp.concatenate(news, axis=0)
    m = jnp.maximum(jnp.max(s, axis=1, keepdims=True), sn)
    p = jnp.exp(s - m)
    pn = jnp.exp(sn - m)
    den = jnp.sum(p, axis=1, keepdims=True) + pn
    lse = m + jnp.log(den)
    for h in range(hb):
        vc = c_ref[1, h]
        vn = vn_ref[h]
        hs = slice(h, h + 1)
        o_ref[h] = (jnp.sum(vc * p[hs, :], axis=1, keepdims=True) + vn * pn[hs, :]) / den[hs, :]
        l_ref[h] = jnp.broadcast_to(lse[hs, :], (HEAD_DIM, 1))
        co_ref[0, h] = jnp.where(last, kn_ref[h], pltpu.roll(c_ref[0, h], n - 1, 1))
        co_ref[1, h] = jnp.where(last, vn, pltpu.roll(vc, n - 1, 1))


def _sample_cache(cache_t, q, kn, vn, gi, dil):
    ns, _, _, _, n = cache_t.shape
    hb = HEADS if n * HEAD_DIM * HEADS * 2 * 4 <= 4 * 1024 * 1024 else HEADS // 2
    cspec = pl.BlockSpec((None, 2, hb, HEAD_DIM, n), lambda b, h: (b, 0, h, 0, 0))
    vspec = pl.BlockSpec((None, None, hb, HEAD_DIM, 1), lambda b, h: (b, gi, h, 0, 0))
    ospec = pl.BlockSpec((None, hb, HEAD_DIM, 1), lambda b, h: (b, h, 0, 0))
    col = jax.ShapeDtypeStruct((ns, HEADS, HEAD_DIM, 1), F32)
    return pl.pallas_call(
        functools.partial(_sample_cache_kernel, hb=hb, n=n, dil=dil),
        grid=(ns, HEADS // hb),
        in_specs=[cspec, vspec, vspec, vspec],
        out_specs=[cspec, ospec, ospec],
        out_shape=[jax.ShapeDtypeStruct(cache_t.shape, F32), col, col],
        compiler_params=_params(("arbitrary", "arbitrary")),
        name="sample_cache",
    )(cache_t, q, kn, vn)


def _experts_kernel(be_ref, bv_ref, x_ref, wg_ref, wu_ref, wd_ref, y_ref, wgb_ref, wub_ref, wdb_ref, *, bm):
    i = pl.program_id(0)
    e = be_ref[i]
    prev = be_ref[jnp.maximum(i - 1, 0)]
    valid = bv_ref[i]

    @pl.when(jnp.logical_or(i == 0, e != prev))
    def _():
        wgb_ref[...] = wg_ref[...].astype(BF16)
        wub_ref[...] = wu_ref[...].astype(BF16)
        wdb_ref[...] = wd_ref[...].astype(BF16)

    def run(rows):
        words = jnp.concatenate([x_ref[c, 0:rows, :] for c in range(ROW_CHUNKS)], axis=1)
        row = lax.broadcasted_iota(I32, (rows, D_MODEL), 0)
        x = jnp.where(row < valid, _unpack_rows(words), 0.0).astype(BF16)
        g = jnp.dot(x, wgb_ref[...], preferred_element_type=F32)
        u = jnp.dot(x, wub_ref[...], preferred_element_type=F32)
        a = (g * _sigmoid(g) * u).astype(BF16)
        y = jnp.dot(a, wdb_ref[...], preferred_element_type=F32)
        yw = _pack_rows(y.astype(BF16))
        for c in range(ROW_CHUNKS):
            y_ref[c, 0:rows, :] = yw[:, c * CHUNK:(c + 1) * CHUNK]
            if rows < bm:
                y_ref[c, rows:bm, :] = jnp.zeros((bm - rows, CHUNK), F32)

    few = EXPERT_BLOCK_FEW

    @pl.when(valid > few)
    def _():
        run(bm)

    @pl.when(jnp.logical_and(valid > 0, valid <= few))
    def _():
        run(few)

    @pl.when(valid <= 0)
    def _():
        y_ref[...] = jnp.zeros((ROW_CHUNKS, bm, CHUNK), F32)


def _experts(blk_e, blk_valid, x_sorted, w_g, w_u, w_d, bm):
    rows = x_sorted.shape[1]
    nblk = rows // bm
    grid_spec = pltpu.PrefetchScalarGridSpec(
        num_scalar_prefetch=2,
        grid=(nblk,),
        in_specs=[pl.BlockSpec((ROW_CHUNKS, bm, CHUNK), lambda i, be, bv: (0, i, 0)),
                  pl.BlockSpec((None, D_MODEL, D_EXPERT), lambda i, be, bv: (be[i], 0, 0)),
                  pl.BlockSpec((None, D_MODEL, D_EXPERT), lambda i, be, bv: (be[i], 0, 0)),
                  pl.BlockSpec((None, D_EXPERT, D_MODEL), lambda i, be, bv: (be[i], 0, 0))],
        out_specs=pl.BlockSpec((ROW_CHUNKS, bm, CHUNK), lambda i, be, bv: (0, i, 0)),
        scratch_shapes=[pltpu.VMEM((D_MODEL, D_EXPERT), BF16), pltpu.VMEM((D_MODEL, D_EXPERT), BF16),
                        pltpu.VMEM((D_EXPERT, D_MODEL), BF16)])
    return pl.pallas_call(
        functools.partial(_experts_kernel, bm=bm),
        grid_spec=grid_spec,
        out_shape=jax.ShapeDtypeStruct((ROW_CHUNKS, rows, CHUNK), F32),
        compiler_params=_params(("arbitrary",)),
        name="experts",
    )(blk_e, blk_valid, x_sorted, w_g, w_u, w_d)


def _final_kernel(x2_ref, modf_ref, wts_ref, yg_ref, gfin_ref, o_ref):
    w = wts_ref[...]

    def picked(k):
        return _unpack_rows(jnp.concatenate([yg_ref[k, c] for c in range(ROW_CHUNKS)], axis=1))

    routed = w[:, 0:1] * picked(0)
    for k in range(1, TOP_K):
        routed = routed + w[:, k:k + 1] * picked(k)
    x = x2_ref[...] + modf_ref[:, 2 * D_MODEL:3 * D_MODEL] * routed
    ms = jnp.mean(x * x, axis=-1, keepdims=True)
    o_ref[...] = x * lax.rsqrt(ms + RMS_EPS) * gfin_ref[...]


def _final(x2, modf, wts, yg, g_final, tok_block_offset, tm):
    B, S, _ = x2.shape
    nt = S // tm
    tok = lambda b, i: (b, i, 0)
    mod_rows = modf.shape[1]
    mod_spec = (pl.BlockSpec((None, 1, 3 * D_MODEL), lambda b, i: (b, 0, 0)) if mod_rows == 1
                else pl.BlockSpec((None, tm, 3 * D_MODEL), tok))
    return pl.pallas_call(
        _final_kernel,
        grid=(B, nt),
        in_specs=[pl.BlockSpec((None, tm, D_MODEL), tok), mod_spec,
                  pl.BlockSpec((tm, TOP_K), lambda b, i: (tok_block_offset + b * nt + i, 0)),
                  pl.BlockSpec((TOP_K, ROW_CHUNKS, tm, CHUNK),
                               lambda b, i: (0, 0, tok_block_offset + b * nt + i, 0)),
                  _const_spec((1, D_MODEL))],
        out_specs=pl.BlockSpec((None, tm, D_MODEL), tok),
        out_shape=jax.ShapeDtypeStruct((B, S, D_MODEL), F32),
        compiler_params=_params(("arbitrary", "arbitrary")),
        name="final",
    )(x2, modf, wts, yg, g_final)


DEST_LANES = 1024


def _dest_kernel(pstart_ref, idx_ref, pos_ref, o_ref):
    idx = idx_ref[...]

    def add_start(e, acc):
        return acc + jnp.where(idx == e, pstart_ref[e], 0)

    o_ref[...] = lax.fori_loop(0, N_EXPERTS, add_start, pos_ref[...])


def _dest(pstart, idx_t, pos_t):
    n = idx_t.shape[1]
    spec = pl.BlockSpec((TOP_K, DEST_LANES), lambda i, ps: (0, i))
    return pl.pallas_call(
        _dest_kernel,
        grid_spec=pltpu.PrefetchScalarGridSpec(num_scalar_prefetch=1, grid=(n // DEST_LANES,),
                                               in_specs=[spec, spec], out_specs=spec),
        out_shape=jax.ShapeDtypeStruct((TOP_K, n), I32),
        compiler_params=_params(("arbitrary",)),
        name="dest",
    )(pstart, idx_t, pos_t)


def _sc_worker_id():
    return lax.axis_index("s") * 2 + lax.axis_index("c")


def _dispatch_rows(h_all, dest_blocks, rows_out):
    nsteps = dest_blocks.shape[0]
    mesh = plsc.VectorSubcoreMesh(core_axis_name="c", subcore_axis_name="s")

    @functools.partial(
        pl.kernel, mesh=mesh,
        out_type=jax.ShapeDtypeStruct((ROW_CHUNKS, rows_out, CHUNK), F32),
        scratch_types=[pltpu.VMEM((TOP_K, SC_WINDOW), I32),
                       pltpu.VMEM((SC_WINDOW, CHUNK), F32),
                       pltpu.SemaphoreType.DMA],
    )
    def k(x_hbm, d_hbm, o_hbm, idx_v, rows_v, sem):
        @pl.loop(_sc_worker_id(), nsteps, step=SC_WORKERS)
        def _(s):
            base = pl.multiple_of(s * SC_WINDOW, SC_WINDOW)
            pltpu.sync_copy(d_hbm.at[s], idx_v)
            for c in range(ROW_CHUNKS):
                pltpu.sync_copy(x_hbm.at[c, pl.ds(base, SC_WINDOW)], rows_v)
                for kk in range(TOP_K):
                    pltpu.async_copy(rows_v, o_hbm.at[c].at[idx_v.at[kk]], sem).wait()

    return k(h_all, dest_blocks)


def _gather_rows(y_sorted, dest_blocks):
    nsteps = dest_blocks.shape[0]
    ntok = nsteps * SC_WINDOW
    mesh = plsc.VectorSubcoreMesh(core_axis_name="c", subcore_axis_name="s")

    @functools.partial(
        pl.kernel, mesh=mesh,
        out_type=jax.ShapeDtypeStruct((TOP_K, ROW_CHUNKS, ntok, CHUNK), F32),
        scratch_types=[pltpu.VMEM((TOP_K, SC_WINDOW), I32),
                       pltpu.VMEM((SC_WINDOW, CHUNK), F32),
                       pltpu.SemaphoreType.DMA],
    )
    def k(y_hbm, d_hbm, o_hbm, idx_v, rows_v, sem):
        @pl.loop(_sc_worker_id(), nsteps, step=SC_WORKERS)
        def _(s):
            base = pl.multiple_of(s * SC_WINDOW, SC_WINDOW)
            pltpu.sync_copy(d_hbm.at[s], idx_v)
            for c in range(ROW_CHUNKS):
                for kk in range(TOP_K):
                    pltpu.async_copy(y_hbm.at[c].at[idx_v.at[kk]], rows_v, sem).wait()
                    pltpu.sync_copy(rows_v, o_hbm.at[kk, c, pl.ds(base, SC_WINDOW)])

    return k(y_sorted, dest_blocks)


PAST_LEN = 8192


def _rope_tables(pos):
    half = HEAD_DIM // 2
    inv_freq = ROPE_THETA ** (-jnp.arange(half, dtype=F32) / half)
    ang = pos.astype(F32)[:, None] * inv_freq[None, :]
    cos, sin = jnp.cos(ang), jnp.sin(ang)
    reps = LANES // HEAD_DIM
    return jnp.tile(cos, (1, 2 * reps)), jnp.tile(jnp.concatenate([-sin, sin], axis=1), (1, reps))


def _stream_order(tab, tm, d):
    s = tab.shape[0]
    return tab.reshape(s // tm, tm // d, d, LANES).transpose(0, 2, 1, 3).reshape(s, LANES)


def _kv_tail(kg, vg, keep):
    b, d, l, _ = kg.shape
    n = keep // d

    def natural(t):
        t = t[:, :, l - n:, :].transpose(0, 2, 1, 3)
        return t.reshape(b, keep, HEADS, HEAD_DIM)

    return jnp.stack([natural(kg), natural(vg)], axis=2).astype(F32)[None]


def kernel(x_prompt, x_sample, cache_kv_w128, cache_kv_w512, cache_kv_w2048, state_conv, c_prompt, c_sample,
           g_mix, w_ada_mix, b_ada_mix, w_in, w_dw, b_dw, ln_conv_g, ln_conv_b, w_conv_out, w_att_out, w_o,
           g_ffn, w_ada_ffn, b_ada_ffn, w_router, b_router, w_exp_gate, w_exp_up, w_exp_down,
           w_sh_gate, w_sh_up, w_sh_down, g_final):
    B, S, _ = x_prompt.shape
    ns, T, _ = x_sample.shape
    assert g_mix.shape[0] == 1 and T == 1
    tm = min(TOKEN_TILE, S)
    span = DILATIONS[2] * 16
    assert S % tm == 0 and tm % span == 0 and S % (DILATIONS[2] * Q_BLOCK) == 0
    assert ns % SC_WINDOW == 0 and (B * S) % ns == 0
    caches = (cache_kv_w128, cache_kv_w512, cache_kv_w2048)

    row = lambda v: v.reshape(1, -1)
    w_in_b = w_in[0].astype(BF16)
    w_co_b = w_conv_out[0].astype(BF16)
    w_ao_b = w_att_out[0].astype(BF16)
    w_o_b = w_o[0].astype(BF16)
    wsg_b = w_sh_gate[0].astype(BF16)
    wsu_b = w_sh_up[0].astype(BF16)
    wsd_b = w_sh_down[0].astype(BF16)

    n_c = B + ns
    c_all = jnp.concatenate([c_prompt, c_sample], axis=0)
    c_all = jnp.pad(c_all, ((0, -n_c % 8), (0, 0)))
    mod_mix = _ada(c_all, w_ada_mix[0], b_ada_mix[0])
    mod_ffn = _ada(c_all, w_ada_ffn[0], b_ada_ffn[0])
    modm_p = mod_mix[:B].reshape(B, 1, 3 * D_MODEL)
    modf_p = mod_ffn[:B].reshape(B, 1, 3 * D_MODEL)
    modm_s = mod_mix[B:n_c].reshape(1, ns, 3 * D_MODEL)
    modf_s = mod_ffn[B:n_c].reshape(1, ns, 3 * D_MODEL)

    cos_p, sin_p = _rope_tables(jnp.arange(S, dtype=I32))
    tables = []
    for d in DILATIONS:
        tables += [_stream_order(cos_p, tm, d), _stream_order(sin_p, tm, d)]
    (q0, k0, v0, q1, k1, v1, q2, k2, v2, gc_p, ga_p, utail) = _inproj_prompt(
        x_prompt, modm_p, row(g_mix[0]), w_in_b, tables, w_dw[0], row(b_dw[0]),
        row(ln_conv_g[0]), row(ln_conv_b[0]), w_co_b, tm)
    conv_prompt = utail[:, CONV_HALO - (CONV_WIDTH - 1):][None]

    attn_in = []
    lse_in = []
    kv_prompt = []
    for (qg, kg, vg), d in zip(((q0, k0, v0), (q1, k1, v1), (q2, k2, v2)), DILATIONS):
        l = S // d
        flat = lambda t: t.reshape(B * d, l, D_SLOT)
        o, lse = _attn_prompt(flat(qg), flat(kg), flat(vg))
        if d == 1:
            attn_in.append(o.reshape(B, S, D_SLOT))
            lse_in.append(lse.reshape(B, S, LANES))
        else:
            attn_in.append(o.reshape(B, d, l, D_SLOT))
            lse_in.append(lse.reshape(B, d, l, LANES))
        keep = min(Q_BLOCK * d, S)
        kv_prompt.append(_kv_tail(kg.reshape(B, d, l, D_SLOT), vg.reshape(B, d, l, D_SLOT), keep))

    wr_t = w_router[0].T
    wr_hi = wr_t.astype(BF16)
    wr_lo = (wr_t - wr_hi.astype(F32)).astype(BF16)
    b_col = b_router[0].reshape(N_EXPERTS, 1)
    zero_carry = jnp.zeros((N_EXPERTS, 1), F32)
    h2_p, x2_p, idx_p, wts_p, pos_p, cnt_p = _post(
        x_prompt, modm_p, (*attn_in, *lse_in), gc_p, ga_p, w_ao_b, w_o_b, row(g_ffn[0]), modf_p,
        wr_hi, wr_lo, b_col, wsg_b, wsu_b, wsd_b, zero_carry, tm, True)

    cos_s, sin_s = _rope_tables(jnp.full((1,), PAST_LEN, I32))
    u_s, q_s, k_s, v_s, sgc_s, ga_s = _inproj_sample(
        x_sample.reshape(ns, D_MODEL), mod_mix[B:n_c], row(g_mix[0]), w_in_b, cos_s, sin_s)
    col = lambda t: t.reshape(ns, N_GROUPS, HEADS, HEAD_DIM, 1)
    q_c, k_c, v_c = col(q_s), col(k_s), col(v_s)
    kv_sample, outs_s, lses_s = [], [], []
    for gi, (cache, d) in enumerate(zip(caches, DILATIONS)):
        cache_t = jnp.transpose(cache[0], (0, 2, 3, 4, 1))
        new_t, o, lse = _sample_cache(cache_t, q_c, k_c, v_c, gi, d)
        kv_sample.append(jnp.transpose(new_t, (0, 4, 1, 2, 3))[None])
        outs_s.append(o.reshape(ns, D_SLOT))
        lses_s.append(lse.reshape(ns, D_SLOT))
    state_t = jnp.transpose(state_conv[0], (1, 0, 2))
    gc_s, att_s, cst_t = _sample_mix(state_t, u_s, sgc_s, w_dw[0], row(b_dw[0]), row(ln_conv_g[0]),
                                     row(ln_conv_b[0]), w_co_b, outs_s, lses_s)
    conv_sample = jnp.transpose(cst_t, (1, 0, 2))[None]
    as3 = lambda t: t.reshape(1, ns, t.shape[-1])
    h2_s, x2_s, idx_s, wts_s, pos_s, cnt = _post(
        as3(x_sample.reshape(ns, D_MODEL)), modm_s, (as3(att_s),), as3(gc_s), as3(ga_s), w_ao_b, w_o_b,
        row(g_ffn[0]), modf_s, wr_hi, wr_lo, b_col, wsg_b, wsu_b, wsd_b, cnt_p, ns, False)

    bm = EXPERT_BLOCK
    ntok = B * S + ns
    nblk = (ntok * TOP_K + N_EXPERTS * (bm - 1)) // bm
    counts = cnt[:, 0].astype(I32)
    padded = (counts + bm - 1) // bm * bm
    pend = jnp.cumsum(padded)
    pstart = pend - padded
    lane_pad = ((0, 0), (0, -ntok % DEST_LANES))
    idx_all = jnp.pad(jnp.concatenate([idx_p, idx_s], axis=1), lane_pad)
    pos_all = jnp.pad(jnp.concatenate([pos_p, pos_s], axis=1), lane_pad)
    dest = _dest(pstart, idx_all, pos_all)[:, :ntok]
    dest_blocks = dest.reshape(TOP_K, ntok // SC_WINDOW, SC_WINDOW).transpose(1, 0, 2)
    wts_tok = jnp.concatenate([wts_p, wts_s], axis=1).T
    blk_row0 = jnp.arange(nblk, dtype=I32) * bm
    blk_e = jnp.minimum(jnp.searchsorted(pend, blk_row0, side='right'), N_EXPERTS - 1).astype(I32)
    blk_valid = jnp.clip(counts[blk_e] - (blk_row0 - pstart[blk_e]), 0, bm).astype(I32)

    h_all = jnp.concatenate([h2_p, h2_s], axis=1)
    x_sorted = _dispatch_rows(h_all, dest_blocks, nblk * bm)
    y_sorted = _experts(blk_e, blk_valid, x_sorted, w_exp_gate[0], w_exp_up[0], w_exp_down[0], bm)
    yg = _gather_rows(y_sorted, dest_blocks)

    y_prompt = _final(x2_p, modf_p, wts_tok, yg, row(g_final), 0, tm)
    y_sample = _final(x2_s, modf_s, wts_tok, yg, row(g_final), (B * S) // ns, ns).reshape(ns, 1, D_MODEL)

    return (y_prompt, y_sample, kv_prompt[0], kv_prompt[1], kv_prompt[2], conv_prompt,
            kv_sample[0], kv_sample[1], kv_sample[2], conv_sample)
```

```python
import functools

import jax
import jax.numpy as jnp
from jax import lax
from jax.experimental import pallas as pl
from jax.experimental.pallas import tpu as pltpu
from jax.experimental.pallas import tpu_sc as plsc

F32 = jnp.float32
BF16 = jnp.bfloat16
I32 = jnp.int32

D_MODEL = 1024
D_CONV = 1024
CONV_WIDTH = 31
HEAD_DIM = 64
HEADS = 8
D_SLOT = HEADS * HEAD_DIM
DILATIONS = (1, 4, 16)
N_GROUPS = 3
D_ATT = N_GROUPS * D_SLOT
Q_BLOCK = 128
ROPE_THETA = 10000.0
N_EXPERTS = 256
TOP_K = 8
N_EXPERT_GROUPS = 8
TOPK_GROUPS = 4
GROUP_SIZE = N_EXPERTS // N_EXPERT_GROUPS
D_EXPERT = 256
ROUTED_SCALE = 2.5
RMS_EPS = 1e-6
LN_EPS = 1e-5

COL_A = 0
COL_B = D_CONV
COL_Q = 2 * D_CONV
COL_K = COL_Q + D_ATT
COL_V = COL_K + D_ATT
COL_GC = COL_V + D_ATT
COL_GA = COL_GC + D_MODEL
D_IN = COL_GA + D_MODEL

LANES = 128
CONV_HALO = 32
TOKEN_TILE = 512
EXPERT_BLOCK = 1152
EXPERT_ROW_STEPS = (128, 1024, 1152)
ROW_CHUNKS = 2
CHUNK = D_MODEL // 2 // ROW_CHUNKS
SC_WORKERS = 32
SC_WINDOW = 128
NEG_BIG = -1e30
VMEM_LIMIT = 56 * 1024 * 1024


def _sigmoid(x):
    return 1.0 / (1.0 + jnp.exp(-x))


def _const_spec(shape):
    nd = len(shape)
    return pl.BlockSpec(shape, lambda *_: (0,) * nd, pipeline_mode=pl.Buffered(1))


def _params(sem):
    return pltpu.CompilerParams(dimension_semantics=sem, vmem_limit_bytes=VMEM_LIMIT)


def _ada_kernel(c_ref, w_ref, b_ref, o_ref):
    c = c_ref[...]
    s = (c * _sigmoid(c)).astype(BF16)
    o_ref[...] = jnp.dot(s, w_ref[...].astype(BF16), preferred_element_type=F32) + b_ref[...]


def _ada(c_all, w, b):
    rows = c_all.shape[0]
    cols = w.shape[1]
    tn = 768
    return pl.pallas_call(
        _ada_kernel,
        grid=(cols // tn,),
        in_specs=[pl.BlockSpec((rows, D_MODEL), lambda j: (0, 0)),
                  pl.BlockSpec((D_MODEL, tn), lambda j: (0, j)),
                  pl.BlockSpec((1, tn), lambda j: (0, j))],
        out_specs=pl.BlockSpec((rows, tn), lambda j: (0, j)),
        out_shape=jax.ShapeDtypeStruct((rows, cols), F32),
        compiler_params=_params(("arbitrary",)),
        name="ada",
    )(c_all, w, b.reshape(1, cols))


def _modulated_norm(x, g, mod):
    shift = mod[:, 0:D_MODEL]
    scale = mod[:, D_MODEL:2 * D_MODEL]
    ms = jnp.mean(x * x, axis=-1, keepdims=True)
    return (x * lax.rsqrt(ms + RMS_EPS)) * g * (1.0 + scale) + shift


def _rope(t, cos, sin_signed):
    lane = lax.broadcasted_iota(I32, (t.shape[0], LANES), 1)
    first_half = (lane & (HEAD_DIM - 1)) < (HEAD_DIM // 2)
    outs = []
    for j in range(D_SLOT // LANES):
        ch = t[:, j * LANES:(j + 1) * LANES]
        rot = jnp.where(first_half, pltpu.roll(ch, LANES - HEAD_DIM // 2, 1), pltpu.roll(ch, HEAD_DIM // 2, 1))
        outs.append(ch * cos + rot * sin_signed)
    return jnp.concatenate(outs, axis=1)


def _layer_norm_swish(y, g, b):
    mu = jnp.mean(y, axis=-1, keepdims=True)
    yc = y - mu
    var = jnp.mean(yc * yc, axis=-1, keepdims=True)
    z = yc * lax.rsqrt(var + LN_EPS) * g + b
    return z * _sigmoid(z)


def _inproj_kernel(x_ref, mod_ref, g_ref, w_ref, base_ref, c0_ref, s0_ref, c1_ref, s1_ref, c2_ref, s2_ref,
                   wdw_ref, bdw_ref, lng_ref, lnb_ref, wco_ref,
                   q0_ref, k0_ref, v0_ref, q1_ref, k1_ref, v1_ref, q2_ref, k2_ref, v2_ref,
                   gc_ref, ga_ref, ut_ref,
                   hf_ref, hb0_ref, hb1_ref, hb2_ref, ubuf_ref, *, tm):
    i = pl.program_id(1)
    cw = 512

    h = _modulated_norm(x_ref[...], g_ref[...], mod_ref[...])
    nlc = D_MODEL // LANES
    for c in range(nlc):
        hf_ref[c] = h[:, c * LANES:(c + 1) * LANES]
    hb0_ref[...] = h.astype(BF16)
    for d, hb in ((DILATIONS[1], hb1_ref), (DILATIONS[2], hb2_ref)):
        n = tm // d
        for r in range(d):
            for c in range(nlc):
                hb[r * n:(r + 1) * n, c * LANES:(c + 1) * LANES] = hf_ref[c, pl.ds(r, n, stride=d), :].astype(BF16)

    def mm(hb, col, width=cw):
        return jnp.dot(hb[...], w_ref[:, col:col + width], preferred_element_type=F32)

    @pl.when(i == 0)
    def _():
        ubuf_ref[:, 0:CONV_HALO, :] = jnp.zeros((nlc, CONV_HALO, LANES), F32)

    for c in range(0, D_CONV, cw):
        a = mm(hb0_ref, COL_A + c)
        b = mm(hb0_ref, COL_B + c)
        u = a * _sigmoid(b)
        for cc in range(cw // LANES):
            ubuf_ref[c // LANES + cc, CONV_HALO:CONV_HALO + tm, :] = u[:, cc * LANES:(cc + 1) * LANES]

    groups = ((hb0_ref, 1, c0_ref, s0_ref, q0_ref, k0_ref, v0_ref),
              (hb1_ref, DILATIONS[1], c1_ref, s1_ref, q1_ref, k1_ref, v1_ref),
              (hb2_ref, DILATIONS[2], c2_ref, s2_ref, q2_ref, k2_ref, v2_ref))
    cos_b = base_ref[0:1, :]
    sin_b = base_ref[1:2, :]
    lane = lax.broadcasted_iota(I32, (1, LANES), 1)
    sign = jnp.where((lane & (HEAD_DIM - 1)) < (HEAD_DIM // 2), -1.0, 1.0)
    for gi, (hb, d, c_ref, s_ref, q_ref, k_ref, v_ref) in enumerate(groups):
        cos_w = c_ref[...]
        sin_w = s_ref[...]
        cos = cos_b * cos_w - sin_b * sin_w
        sin = (sin_b * cos_w + cos_b * sin_w) * sign
        n = tm // d

        def put(ref, val):
            vb = val.astype(BF16)
            if d == 1:
                ref[...] = vb
            else:
                for r in range(d):
                    ref[r] = vb[r * n:(r + 1) * n, :]

        put(q_ref, _rope(mm(hb, COL_Q + gi * D_SLOT), cos, sin) * (HEAD_DIM ** -0.5))
        put(k_ref, _rope(mm(hb, COL_K + gi * D_SLOT), cos, sin))
        put(v_ref, mm(hb, COL_V + gi * D_SLOT))

    for c in range(0, D_MODEL, cw):
        ga_ref[:, c:c + cw] = _sigmoid(mm(hb0_ref, COL_GA + c)).astype(BF16)

    rb = 32
    first_tap = CONV_HALO - (CONV_WIDTH - 1)

    def conv_rows(r, carry):
        r0 = pl.multiple_of(r * rb, rb)
        for c in range(nlc):
            cs = slice(c * LANES, (c + 1) * LANES)
            acc = jnp.broadcast_to(bdw_ref[:, cs], (rb, LANES))
            for k in range(CONV_WIDTH):
                acc = acc + wdw_ref[k:k + 1, cs] * ubuf_ref[c, pl.ds(r0 + (first_tap + k), rb), :]
            hf_ref[c, pl.ds(r0, rb), :] = acc
        return carry

    lax.fori_loop(0, tm // rb, conv_rows, 0)

    y_dw = jnp.concatenate([hf_ref[c] for c in range(nlc)], axis=1)
    z = _layer_norm_swish(y_dw, lng_ref[...], lnb_ref[...]).astype(BF16)
    for c in range(0, D_MODEL, cw):
        co = jnp.dot(z, wco_ref[:, c:c + cw], preferred_element_type=F32)
        gc_ref[:, c:c + cw] = (_sigmoid(mm(hb0_ref, COL_GC + c)) * co).astype(BF16)

    for c in range(nlc):
        tail = ubuf_ref[c, tm:tm + CONV_HALO, :]
        ut_ref[:, c * LANES:(c + 1) * LANES] = tail
        ubuf_ref[c, 0:CONV_HALO, :] = tail


def _inproj_prompt(x, mod, g, w_in_b, tables, w_dw, b_dw, ln_g, ln_b, w_co_b, tm):
    B, S, _ = x.shape
    nt = S // tm
    d1, d2 = DILATIONS[1], DILATIONS[2]
    tok = lambda b, i: (b, i, 0)
    tab = _const_spec((tm, LANES))
    in_specs = [
        pl.BlockSpec((None, tm, D_MODEL), tok),
        pl.BlockSpec((None, 1, 3 * D_MODEL), lambda b, i: (b, 0, 0)),
        _const_spec((1, D_MODEL)),
        _const_spec((D_MODEL, D_IN)),
        pl.BlockSpec((None, 2, LANES), lambda b, i: (i, 0, 0)),
        tab, tab, tab, tab, tab, tab,
        _const_spec((CONV_WIDTH, D_CONV)),
        _const_spec((1, D_CONV)),
        _const_spec((1, D_CONV)),
        _const_spec((1, D_CONV)),
        _const_spec((D_CONV, D_MODEL)),
    ]
    nat = pl.BlockSpec((None, tm, D_SLOT), tok)
    st1 = pl.BlockSpec((None, d1, tm // d1, D_SLOT), lambda b, i: (b, 0, i, 0))
    st2 = pl.BlockSpec((None, d2, tm // d2, D_SLOT), lambda b, i: (b, 0, i, 0))
    out_specs = [nat, nat, nat, st1, st1, st1, st2, st2, st2,
                 pl.BlockSpec((None, tm, D_MODEL), tok),
                 pl.BlockSpec((None, tm, D_MODEL), tok),
                 pl.BlockSpec((None, CONV_HALO, D_CONV), lambda b, i: (b, 0, 0))]
    s0 = jax.ShapeDtypeStruct((B, S, D_SLOT), BF16)
    s1 = jax.ShapeDtypeStruct((B, d1, S // d1, D_SLOT), BF16)
    s2 = jax.ShapeDtypeStruct((B, d2, S // d2, D_SLOT), BF16)
    out_shape = [s0, s0, s0, s1, s1, s1, s2, s2, s2,
                 jax.ShapeDtypeStruct((B, S, D_MODEL), BF16),
                 jax.ShapeDtypeStruct((B, S, D_MODEL), BF16),
                 jax.ShapeDtypeStruct((B, CONV_HALO, D_CONV), F32)]
    scratch = [pltpu.VMEM((D_MODEL // LANES, tm, LANES), F32),
               pltpu.VMEM((tm, D_MODEL), BF16),
               pltpu.VMEM((tm, D_MODEL), BF16),
               pltpu.VMEM((tm, D_MODEL), BF16),
               pltpu.VMEM((D_CONV // LANES, tm + CONV_HALO, LANES), F32)]
    return pl.pallas_call(
        functools.partial(_inproj_kernel, tm=tm),
        grid=(B, nt),
        in_specs=in_specs, out_specs=out_specs, out_shape=out_shape,
        scratch_shapes=scratch,
        compiler_params=_params(("arbitrary", "arbitrary")),
        name="inproj",
    )(x, mod, g, w_in_b, *tables, w_dw, b_dw, ln_g, ln_b, w_co_b)


def _attn_kernel(q_ref, kp_ref, kc_ref, vp_ref, vc_ref, o_ref, lse_ref, k_all, v_all, *, nq):
    j = pl.program_id(1)
    k_all[0:Q_BLOCK, :] = kp_ref[...]
    k_all[Q_BLOCK:(nq + 1) * Q_BLOCK, :] = kc_ref[...]
    v_all[0:Q_BLOCK, :] = vp_ref[...]
    v_all[Q_BLOCK:(nq + 1) * Q_BLOCK, :] = vc_ref[...]
    qi = lax.broadcasted_iota(I32, (Q_BLOCK, 2 * Q_BLOCK), 0)
    ki = lax.broadcasted_iota(I32, (Q_BLOCK, 2 * Q_BLOCK), 1)
    back = Q_BLOCK + qi - ki
    in_band = jnp.logical_and(back >= 0, back <= Q_BLOCK)
    first_mask = jnp.logical_and(in_band, jnp.logical_or(ki >= Q_BLOCK, j > 0))
    lane = lax.broadcasted_iota(I32, (Q_BLOCK, LANES), 1)
    even = lax.broadcasted_iota(I32, (1, LANES), 1) < HEAD_DIM
    zero = jnp.zeros((), BF16)
    nt = (((1,), (1,)), ((), ()))
    for sub in range(nq):
        mask = first_mask if sub == 0 else in_band
        qrows = slice(sub * Q_BLOCK, (sub + 1) * Q_BLOCK)
        krows = slice(sub * Q_BLOCK, (sub + 2) * Q_BLOCK)
        scores = []
        for h in range(HEADS):
            pair = slice((h // 2) * LANES, (h // 2 + 1) * LANES)
            mine = even if h % 2 == 0 else jnp.logical_not(even)
            qh = jnp.where(mine, q_ref[qrows, pair], zero)
            scores.append(lax.dot_general(qh, k_all[krows, pair], nt, preferred_element_type=F32))
        probs, dens = [], []
        lse_all = jnp.zeros((Q_BLOCK, LANES), F32)
        for h in range(HEADS):
            s = jnp.where(mask, scores[h], NEG_BIG)
            m = jnp.max(s, axis=-1, keepdims=True)
            p = jnp.exp(s - m)
            den = jnp.sum(p, axis=-1, keepdims=True)
            probs.append(p.astype(BF16))
            dens.append(den)
            lse_all = jnp.where(lane == h, m + jnp.log(den), lse_all)
        for hp in range(HEADS // 2):
            pair = slice(hp * LANES, (hp + 1) * LANES)
            v2 = v_all[krows, pair]
            oe = jnp.dot(probs[2 * hp], v2, preferred_element_type=F32) / dens[2 * hp]
            oo = jnp.dot(probs[2 * hp + 1], v2, preferred_element_type=F32) / dens[2 * hp + 1]
            o_ref[qrows, pair] = jnp.where(even, oe, oo).astype(BF16)
        lse_ref[qrows, :] = lse_all


def _attn_prompt(q, k, v):
    ns, L, _ = q.shape
    nq = max(c for c in (4, 2, 1) if (L // Q_BLOCK) % c == 0)
    rows = nq * Q_BLOCK
    cur = pl.BlockSpec((None, rows, D_SLOT), lambda n, j: (n, j, 0))
    prev = pl.BlockSpec((None, Q_BLOCK, D_SLOT), lambda n, j: (n, jnp.maximum(j * nq - 1, 0), 0))
    return pl.pallas_call(
        functools.partial(_attn_kernel, nq=nq),
        grid=(ns, L // rows),
        in_specs=[cur, prev, cur, prev, cur],
        out_specs=[cur, pl.BlockSpec((None, rows, LANES), lambda n, j: (n, j, 0))],
        out_shape=[jax.ShapeDtypeStruct((ns, L, D_SLOT), BF16),
                   jax.ShapeDtypeStruct((ns, L, LANES), F32)],
        scratch_shapes=[pltpu.VMEM((rows + Q_BLOCK, D_SLOT), BF16), pltpu.VMEM((rows + Q_BLOCK, D_SLOT), BF16)],
        compiler_params=_params(("arbitrary", "arbitrary")),
        name="attn",
    )(q, k, k, v, v)


def _route_t(logits_t, b_col, carry, tm):
    ninf = -jnp.inf
    scores = _sigmoid(logits_t)
    sel = scores + b_col
    rowf = lax.broadcasted_iota(I32, (N_EXPERTS, tm), 0).astype(F32)
    past_end = float(N_EXPERTS)

    def first_max(x, rows):
        m = jnp.max(x, axis=0, keepdims=True)
        return m, jnp.min(jnp.where(x == m, rows, past_end), axis=0, keepdims=True)

    gs = []
    rowg = lax.broadcasted_iota(I32, (GROUP_SIZE, tm), 0).astype(F32)
    for g in range(N_EXPERT_GROUPS):
        rs = slice(g * GROUP_SIZE, (g + 1) * GROUP_SIZE)
        m1, i1 = first_max(sel[rs], rowg)
        m2 = jnp.max(jnp.where(rowg == i1, ninf, sel[rs]), axis=0, keepdims=True)
        gs.append(m1 + m2)
    pieces = []
    for g in range(N_EXPERT_GROUPS):
        beaten = jnp.zeros((1, tm), F32)
        for g2 in range(N_EXPERT_GROUPS):
            if g2 != g:
                better = gs[g2] >= gs[g] if g2 < g else gs[g2] > gs[g]
                beaten = beaten + jnp.where(better, 1.0, 0.0)
        rs = slice(g * GROUP_SIZE, (g + 1) * GROUP_SIZE)
        pieces.append(jnp.where(beaten < TOPK_GROUPS, sel[rs], ninf))
    selm = jnp.concatenate(pieces, axis=0)

    idx_rows, w_rows = [], []
    picked = jnp.zeros((N_EXPERTS, tm), F32)
    for _ in range(TOP_K):
        _, ik = first_max(selm, rowf)
        hit = rowf == ik
        w_rows.append(jnp.sum(jnp.where(hit, scores, 0.0), axis=0, keepdims=True))
        selm = jnp.where(hit, ninf, selm)
        picked = jnp.where(hit, 1.0, picked)
        idx_rows.append(ik)
    wsum = w_rows[0]
    for wk in w_rows[1:]:
        wsum = wsum + wk
    denom = wsum + 1e-20

    ti = lax.broadcasted_iota(I32, (tm, tm), 0)
    tj = lax.broadcasted_iota(I32, (tm, tm), 1)
    earlier = jnp.where(ti < tj, 1.0, 0.0).astype(BF16)
    before = jnp.dot(picked.astype(BF16), earlier, preferred_element_type=F32) + carry
    new_carry = carry + jnp.sum(picked, axis=1, keepdims=True)
    pos_rows = [jnp.sum(jnp.where(rowf == ik, before, 0.0), axis=0, keepdims=True) for ik in idx_rows]

    idx_o = jnp.concatenate(idx_rows, axis=0).astype(I32)
    w_o = jnp.concatenate([wk / denom * ROUTED_SCALE for wk in w_rows], axis=0)
    pos_o = jnp.concatenate(pos_rows, axis=0).astype(I32)
    return idx_o, w_o, pos_o, new_carry


def _pack_rows(xb):
    bits = lax.bitcast_convert_type(xb.astype(F32), jnp.uint32)
    half = D_MODEL // 2
    word = bits[:, half:] | (bits[:, :half] >> 16)
    return lax.bitcast_convert_type(word, F32)


def _unpack_rows(words):
    bits = lax.bitcast_convert_type(words, jnp.uint32)
    lo = lax.bitcast_convert_type(bits << 16, F32)
    hi = lax.bitcast_convert_type(bits & jnp.uint32(0xFFFF0000), F32)
    return jnp.concatenate([lo, hi], axis=1)


def _post_kernel(*refs, tm, combine):
    refs = list(refs)
    x_ref, modm_ref = refs[0:2]
    p = 2
    if combine:
        o0_ref, o1_ref, o2_ref, l0_ref, l1_ref, l2_ref = refs[p:p + 6]
        p += 6
    else:
        att_ref = refs[p]
        p += 1
    (gc_ref, ga_ref, wao_ref, wo_ref, gffn_ref, modf_ref, wrh_ref, wrl_ref, br_ref,
     wsg_ref, wsu_ref, wsd_ref, cin_ref) = refs[p:p + 13]
    p += 13
    h2_ref, x2_ref, idx_ref, wts_ref, pos_ref, cnt_ref = refs[p:p + 6]
    p += 6
    if combine:
        o1n_ref, o2n_ref, l1n_ref, l2n_ref = refs[p:p + 4]

    first = jnp.logical_and(pl.program_id(0) == 0, pl.program_id(1) == 0)

    @pl.when(first)
    def _():
        cnt_ref[...] = cin_ref[...]

    if combine:
        nsc = D_SLOT // LANES
        for d, o_ref, l_ref, on_ref, ln_ref in ((DILATIONS[1], o1_ref, l1_ref, o1n_ref, l1n_ref),
                                                (DILATIONS[2], o2_ref, l2_ref, o2n_ref, l2n_ref)):
            n = tm // d
            for r in range(d):
                ln_ref[pl.ds(r, n, stride=d), :] = l_ref[r]
                orow = o_ref[r].astype(F32)
                for c in range(nsc):
                    on_ref[c, pl.ds(r, n, stride=d), :] = orow[:, c * LANES:(c + 1) * LANES]
        l0 = l0_ref[...]
        l1 = l1n_ref[...]
        l2 = l2n_ref[...]
        mx = jnp.maximum(jnp.maximum(l0, l1), l2)
        e0 = jnp.exp(l0 - mx)
        e1 = jnp.exp(l1 - mx)
        e2 = jnp.exp(l2 - mx)
        esum = e0 + e1 + e2
        even = lax.broadcasted_iota(I32, (tm, LANES), 1) < HEAD_DIM

        def expand(w):
            wide = lambda h: jnp.broadcast_to(w[:, h:h + 1], (tm, LANES))
            return jnp.concatenate([jnp.where(even, wide(2 * c), wide(2 * c + 1)) for c in range(nsc)], axis=1)

        o1n = jnp.concatenate([o1n_ref[c] for c in range(nsc)], axis=1)
        o2n = jnp.concatenate([o2n_ref[c] for c in range(nsc)], axis=1)
        att = (expand(e0 / esum) * o0_ref[...].astype(F32) + expand(e1 / esum) * o1n
               + expand(e2 / esum) * o2n)
    else:
        att = att_ref[...]

    att_out = jnp.dot(att.astype(BF16), wao_ref[...], preferred_element_type=F32)
    merged = gc_ref[...].astype(F32) + ga_ref[...].astype(F32) * att_out
    y = jnp.dot(merged.astype(BF16), wo_ref[...], preferred_element_type=F32)
    x1 = x_ref[...] + modm_ref[:, 2 * D_MODEL:3 * D_MODEL] * y

    modf = modf_ref[...]
    h2 = _modulated_norm(x1, gffn_ref[...], modf)
    h2b = h2.astype(BF16)
    words = _pack_rows(h2b)
    for c in range(ROW_CHUNKS):
        h2_ref[c] = words[:, c * CHUNK:(c + 1) * CHUNK]
    sg = jnp.dot(h2b, wsg_ref[...], preferred_element_type=F32)
    su = jnp.dot(h2b, wsu_ref[...], preferred_element_type=F32)
    sh = jnp.dot((sg * _sigmoid(sg) * su).astype(BF16), wsd_ref[...], preferred_element_type=F32)
    x2_ref[...] = x1 + modf[:, 2 * D_MODEL:3 * D_MODEL] * sh

    h2l = (h2 - h2b.astype(F32)).astype(BF16)
    nt = (((1,), (1,)), ((), ()))
    logits_t = (lax.dot_general(wrh_ref[...], h2b, nt, preferred_element_type=F32)
                + lax.dot_general(wrl_ref[...], h2b, nt, preferred_element_type=F32)
                + lax.dot_general(wrh_ref[...], h2l, nt, preferred_element_type=F32))
    idx_o, w_o, pos_o, new_carry = _route_t(logits_t, br_ref[...], cnt_ref[...], tm)
    idx_ref[...] = idx_o
    wts_ref[...] = w_o
    pos_ref[...] = pos_o
    cnt_ref[...] = new_carry


def _post(x, modm, attn_inputs, gc, ga, w_ao_b, w_o_b, g_ffn, modf, wr_hi, wr_lo, b_router,
          wsg_b, wsu_b, wsd_b, carry_in, tm, combine):
    B, S, _ = x.shape
    nt = S // tm
    tok = lambda b, i: (b, i, 0)
    mod_rows = modm.shape[1]
    mod_spec = (pl.BlockSpec((None, 1, 3 * D_MODEL), lambda b, i: (b, 0, 0)) if mod_rows == 1
                else pl.BlockSpec((None, tm, 3 * D_MODEL), tok))
    full = pl.BlockSpec((None, tm, D_MODEL), tok)
    in_specs = [full, mod_spec]
    scratch = []
    if combine:
        d1, d2 = DILATIONS[1], DILATIONS[2]
        in_specs += [pl.BlockSpec((None, tm, D_SLOT), tok),
                     pl.BlockSpec((None, d1, tm // d1, D_SLOT), lambda b, i: (b, 0, i, 0)),
                     pl.BlockSpec((None, d2, tm // d2, D_SLOT), lambda b, i: (b, 0, i, 0)),
                     pl.BlockSpec((None, tm, LANES), tok),
                     pl.BlockSpec((None, d1, tm // d1, LANES), lambda b, i: (b, 0, i, 0)),
                     pl.BlockSpec((None, d2, tm // d2, LANES), lambda b, i: (b, 0, i, 0))]
        scratch = [pltpu.VMEM((D_SLOT // LANES, tm, LANES), F32),
                   pltpu.VMEM((D_SLOT // LANES, tm, LANES), F32),
                   pltpu.VMEM((tm, LANES), F32),
                   pltpu.VMEM((tm, LANES), F32)]
    else:
        in_specs += [pl.BlockSpec((None, tm, D_SLOT), tok)]
    in_specs += [full, full,
                 _const_spec((D_SLOT, D_MODEL)), _const_spec((D_MODEL, D_MODEL)),
                 _const_spec((1, D_MODEL)), mod_spec,
                 _const_spec((N_EXPERTS, D_MODEL)), _const_spec((N_EXPERTS, D_MODEL)),
                 _const_spec((N_EXPERTS, 1)),
                 _const_spec((D_MODEL, D_EXPERT)), _const_spec((D_MODEL, D_EXPERT)),
                 _const_spec((D_EXPERT, D_MODEL)), _const_spec((N_EXPERTS, 1))]
    pack = pl.BlockSpec((TOP_K, tm), lambda b, i: (0, b * nt + i))
    chunked = pl.BlockSpec((ROW_CHUNKS, tm, CHUNK), lambda b, i: (0, b * nt + i, 0))
    out_specs = [chunked, full, pack, pack, pack, pl.BlockSpec((N_EXPERTS, 1), lambda b, i: (0, 0))]
    out_shape = [jax.ShapeDtypeStruct((ROW_CHUNKS, B * S, CHUNK), F32),
                 jax.ShapeDtypeStruct((B, S, D_MODEL), F32),
                 jax.ShapeDtypeStruct((TOP_K, B * S), I32),
                 jax.ShapeDtypeStruct((TOP_K, B * S), F32),
                 jax.ShapeDtypeStruct((TOP_K, B * S), I32),
                 jax.ShapeDtypeStruct((N_EXPERTS, 1), F32)]
    return pl.pallas_call(
        functools.partial(_post_kernel, tm=tm, combine=combine),
        grid=(B, nt),
        in_specs=in_specs, out_specs=out_specs, out_shape=out_shape,
        scratch_shapes=scratch,
        compiler_params=_params(("arbitrary", "arbitrary")),
        name="post_prompt" if combine else "post_sample",
    )(x, modm, *attn_inputs, gc, ga, w_ao_b, w_o_b, g_ffn, modf, wr_hi, wr_lo, b_router,
      wsg_b, wsu_b, wsd_b, carry_in)


def _inproj_sample_kernel(x_ref, mod_ref, g_ref, w_ref, cos_ref, sin_ref,
                          u_ref, q_ref, k_ref, v_ref, sgc_ref, ga_ref):
    cw = 512
    hb = _modulated_norm(x_ref[...], g_ref[...], mod_ref[...]).astype(BF16)

    def mm(col, width=cw):
        return jnp.dot(hb, w_ref[:, col:col + width], preferred_element_type=F32)

    cos = cos_ref[...]
    sin = sin_ref[...]
    for c in range(0, D_CONV, cw):
        u_ref[:, c:c + cw] = mm(COL_A + c) * _sigmoid(mm(COL_B + c))
        sgc_ref[:, c:c + cw] = _sigmoid(mm(COL_GC + c))
        ga_ref[:, c:c + cw] = _sigmoid(mm(COL_GA + c)).astype(BF16)
    for gi in range(N_GROUPS):
        cs = slice(gi * D_SLOT, (gi + 1) * D_SLOT)
        q_ref[:, cs] = _rope(mm(COL_Q + gi * D_SLOT), cos, sin) * (HEAD_DIM ** -0.5)
        k_ref[:, cs] = _rope(mm(COL_K + gi * D_SLOT), cos, sin)
        v_ref[:, cs] = mm(COL_V + gi * D_SLOT)


def _inproj_sample(x, mod, g, w_in_b, cos, sin):
    ns = x.shape[0]
    whole = lambda shape: pl.BlockSpec(shape, lambda i: (0,) * len(shape))
    f = lambda cols, dt=F32: jax.ShapeDtypeStruct((ns, cols), dt)
    return pl.pallas_call(
        _inproj_sample_kernel,
        grid=(1,),
        in_specs=[whole((ns, D_MODEL)), whole((ns, 3 * D_MODEL)), _const_spec((1, D_MODEL)),
                  _const_spec((D_MODEL, D_IN)), whole((1, LANES)), whole((1, LANES))],
        out_specs=[whole((ns, D_CONV)), whole((ns, D_ATT)), whole((ns, D_ATT)), whole((ns, D_ATT)),
                   whole((ns, D_MODEL)), whole((ns, D_MODEL))],
        out_shape=[f(D_CONV), f(D_ATT), f(D_ATT), f(D_ATT), f(D_MODEL), f(D_MODEL, BF16)],
        compiler_params=_params(("arbitrary",)),
        name="inproj_sample",
    )(x, mod, g, w_in_b, cos, sin)


def _sample_mix_kernel(state_ref, u_ref, sgc_ref, wdw_ref, bdw_ref, lng_ref, lnb_ref, wco_ref,
                       o0_ref, o1_ref, o2_ref, l0_ref, l1_ref, l2_ref,
                       gc_ref, att_ref, cst_ref):
    nprev = CONV_WIDTH - 1
    u = u_ref[...]
    y = wdw_ref[nprev:CONV_WIDTH, :] * u + bdw_ref[...]
    for k in range(nprev):
        y = y + wdw_ref[k:k + 1, :] * state_ref[k]
    z = _layer_norm_swish(y, lng_ref[...], lnb_ref[...]).astype(BF16)
    co = jnp.dot(z, wco_ref[...], preferred_element_type=F32)
    gc_ref[...] = (sgc_ref[...] * co).astype(BF16)
    for k in range(nprev - 1):
        cst_ref[k] = state_ref[k + 1]
    cst_ref[nprev - 1] = u

    l0 = l0_ref[...]
    l1 = l1_ref[...]
    l2 = l2_ref[...]
    mx = jnp.maximum(jnp.maximum(l0, l1), l2)
    e0 = jnp.exp(l0 - mx)
    e1 = jnp.exp(l1 - mx)
    e2 = jnp.exp(l2 - mx)
    esum = e0 + e1 + e2
    att_ref[...] = (e0 / esum) * o0_ref[...] + (e1 / esum) * o1_ref[...] + (e2 / esum) * o2_ref[...]


def _sample_mix(state, u, sgc, w_dw, b_dw, ln_g, ln_b, w_co_b, outs, lses):
    ns = state.shape[1]
    nprev = CONV_WIDTH - 1
    whole = lambda *shape: pl.BlockSpec(shape, lambda i: (0,) * len(shape))
    slot = whole(ns, D_SLOT)
    return pl.pallas_call(
        _sample_mix_kernel,
        grid=(1,),
        in_specs=[whole(nprev, ns, D_CONV), whole(ns, D_CONV), whole(ns, D_MODEL),
                  whole(CONV_WIDTH, D_CONV), whole(1, D_CONV), whole(1, D_CONV), whole(1, D_CONV),
                  whole(D_CONV, D_MODEL), slot, slot, slot, slot, slot, slot],
        out_specs=[whole(ns, D_MODEL), slot, whole(nprev, ns, D_CONV)],
        out_shape=[jax.ShapeDtypeStruct((ns, D_MODEL), BF16),
                   jax.ShapeDtypeStruct((ns, D_SLOT), F32),
                   jax.ShapeDtypeStruct((nprev, ns, D_CONV), F32)],
        compiler_params=_params(("arbitrary",)),
        name="sample_mix",
    )(state, u, sgc, w_dw, b_dw, ln_g, ln_b, w_co_b, *outs, *lses)


def _split3(x):
    p1 = x.astype(BF16)
    r1 = x - p1.astype(F32)
    p2 = r1.astype(BF16)
    p3 = (r1 - p2.astype(F32)).astype(BF16)
    return p1, p2, p3


def _sample_cache_kernel(c_ref, r_ref, co_ref, o_ref, l_ref, *, n, dil, sb):
    lane = lax.broadcasted_iota(I32, (1, n), 1)
    back = n - lane
    use = jnp.logical_and(jnp.logical_and(back % dil == 0, back <= Q_BLOCK * dil), back >= dil)
    last = lax.broadcasted_iota(I32, (HEAD_DIM, n), 1) == n - 1
    nt = (((1,), (1,)), ((), ()))
    eye_d = (lax.broadcasted_iota(I32, (HEAD_DIM, HEAD_DIM), 0)
             == lax.broadcasted_iota(I32, (HEAD_DIM, HEAD_DIM), 1)).astype(BF16)
    pick_h = (lax.broadcasted_iota(I32, (sb * HEADS, LANES), 0)
              == lax.broadcasted_iota(I32, (sb * HEADS, LANES), 1)).astype(BF16)
    lane_h = lax.broadcasted_iota(I32, (HEAD_DIM, LANES), 1)

    all_rows = r_ref[...].reshape(sb * 3 * HEADS, HEAD_DIM)
    all_cols = sum(lax.dot_general(eye_d, part, nt, preferred_element_type=F32) for part in _split3(all_rows))
    o_t = jnp.zeros((HEAD_DIM, LANES), F32)

    for b in range(sb):
        col = lambda kind, h: all_cols[:, (b * 3 + kind) * HEADS + h:(b * 3 + kind) * HEADS + h + 1]
        q_c, kn_c, vn_c = 0, 1, 2
        rows, news = [], []
        for h in range(HEADS):
            rows.append(jnp.sum(c_ref[b, 0, h] * col(q_c, h), axis=0, keepdims=True))
            news.append(jnp.sum(col(kn_c, h) * col(q_c, h), axis=0, keepdims=True))
        s = jnp.where(use, jnp.concatenate(rows, axis=0), NEG_BIG)
        sn = jnp.concatenate(news, axis=0)
        m = jnp.maximum(jnp.max(s, axis=1, keepdims=True), sn)
        p = jnp.exp(s - m)
        pn = jnp.exp(sn - m)
        den = jnp.sum(p, axis=1, keepdims=True) + pn
        for h in range(HEADS):
            vc = c_ref[b, 1, h]
            hs = slice(h, h + 1)
            o_h = (jnp.sum(vc * p[hs, :], axis=1, keepdims=True) + col(vn_c, h) * pn[hs, :]) / den[hs, :]
            o_t = jnp.where(lane_h == b * HEADS + h, o_h, o_t)
            co_ref[b, 0, h] = jnp.where(last, col(kn_c, h), pltpu.roll(c_ref[b, 0, h], n - 1, 1))
            co_ref[b, 1, h] = jnp.where(last, col(vn_c, h), pltpu.roll(vc, n - 1, 1))
        l_ref[b] = jnp.broadcast_to(m + jnp.log(den), (HEADS, HEAD_DIM))
    o_rows = sum(lax.dot_general(pick_h, part, nt, preferred_element_type=F32) for part in _split3(o_t))
    o_ref[...] = o_rows.reshape(sb, HEADS, HEAD_DIM)


def _sample_cache(cache_t, qkv_rows, gi, dil):
    ns, _, _, _, n = cache_t.shape
    seq_bytes = 2 * HEADS * HEAD_DIM * n * 4
    sb = max(1, min(4, (4 * 1024 * 1024) // seq_bytes))
    assert ns % sb == 0
    cspec = pl.BlockSpec((sb, 2, HEADS, HEAD_DIM, n), lambda b: (b, 0, 0, 0, 0))
    rspec = pl.BlockSpec((sb, None, 3, HEADS, HEAD_DIM), lambda b: (b, gi, 0, 0, 0))
    ospec = pl.BlockSpec((sb, HEADS, HEAD_DIM), lambda b: (b, 0, 0))
    slot = jax.ShapeDtypeStruct((ns, HEADS, HEAD_DIM), F32)
    return pl.pallas_call(
        functools.partial(_sample_cache_kernel, n=n, dil=dil, sb=sb),
        grid=(ns // sb,),
        in_specs=[cspec, rspec],
        out_specs=[cspec, ospec, ospec],
        out_shape=[jax.ShapeDtypeStruct(cache_t.shape, F32), slot, slot],
        compiler_params=_params(("arbitrary",)),
        name="sample_cache",
    )(cache_t, qkv_rows)


def _experts_kernel(be_ref, bv_ref, nu_ref, x_ref, wg_ref, wu_ref, wd_ref, y_ref, wgb_ref, wub_ref, wdb_ref, *, bm):
    i = pl.program_id(0)
    e = be_ref[i]
    prev = be_ref[jnp.maximum(i - 1, 0)]
    valid = jnp.where(i < nu_ref[0], bv_ref[i], 0)

    @pl.when(jnp.logical_and(valid > 0, jnp.logical_or(i == 0, e != prev)))
    def _():
        wgb_ref[...] = wg_ref[...].astype(BF16)
        wub_ref[...] = wu_ref[...].astype(BF16)
        wdb_ref[...] = wd_ref[...].astype(BF16)

    def run(rows):
        words = jnp.concatenate([x_ref[c, 0:rows, :] for c in range(ROW_CHUNKS)], axis=1)
        row = lax.broadcasted_iota(I32, (rows, D_MODEL), 0)
        x = jnp.where(row < valid, _unpack_rows(words), 0.0).astype(BF16)
        g = jnp.dot(x, wgb_ref[...], preferred_element_type=F32)
        u = jnp.dot(x, wub_ref[...], preferred_element_type=F32)
        a = (g * _sigmoid(g) * u).astype(BF16)
        y = jnp.dot(a, wdb_ref[...], preferred_element_type=F32)
        yw = _pack_rows(y.astype(BF16))
        for c in range(ROW_CHUNKS):
            y_ref[c, 0:rows, :] = yw[:, c * CHUNK:(c + 1) * CHUNK]
            if rows < bm:
                y_ref[c, rows:bm, :] = jnp.zeros((bm - rows, CHUNK), F32)

    lower = 0
    for rows in EXPERT_ROW_STEPS:
        upper_ok = valid <= rows if rows < bm else True

        @pl.when(jnp.logical_and(valid > lower, upper_ok))
        def _(rows=rows):
            run(rows)

        lower = rows


def _experts(blk_e, blk_valid, n_used, x_sorted, w_g, w_u, w_d, bm):
    rows = x_sorted.shape[1]
    nblk = rows // bm
    blk = lambda i, be, bv, nu: (0, jnp.minimum(i, nu[0] - 1), 0)
    wsel = lambda i, be, bv, nu: (be[jnp.minimum(i, nu[0] - 1)], 0, 0)
    grid_spec = pltpu.PrefetchScalarGridSpec(
        num_scalar_prefetch=3,
        grid=(nblk,),
        in_specs=[pl.BlockSpec((ROW_CHUNKS, bm, CHUNK), blk),
                  pl.BlockSpec((None, D_MODEL, D_EXPERT), wsel),
                  pl.BlockSpec((None, D_MODEL, D_EXPERT), wsel),
                  pl.BlockSpec((None, D_EXPERT, D_MODEL), wsel)],
        out_specs=pl.BlockSpec((ROW_CHUNKS, bm, CHUNK), blk),
        scratch_shapes=[pltpu.VMEM((D_MODEL, D_EXPERT), BF16), pltpu.VMEM((D_MODEL, D_EXPERT), BF16),
                        pltpu.VMEM((D_EXPERT, D_MODEL), BF16)])
    return pl.pallas_call(
        functools.partial(_experts_kernel, bm=bm),
        grid_spec=grid_spec,
        out_shape=jax.ShapeDtypeStruct((ROW_CHUNKS, rows, CHUNK), F32),
        compiler_params=_params(("arbitrary",)),
        name="experts",
    )(blk_e, blk_valid, n_used, x_sorted, w_g, w_u, w_d)


def _final_kernel(x2_ref, modf_ref, wts_ref, yg_ref, gfin_ref, o_ref):
    w = wts_ref[...]

    def picked(k):
        return _unpack_rows(jnp.concatenate([yg_ref[k, c] for c in range(ROW_CHUNKS)], axis=1))

    routed = w[:, 0:1] * picked(0)
    for k in range(1, TOP_K):
        routed = routed + w[:, k:k + 1] * picked(k)
    x = x2_ref[...] + modf_ref[:, 2 * D_MODEL:3 * D_MODEL] * routed
    ms = jnp.mean(x * x, axis=-1, keepdims=True)
    o_ref[...] = x * lax.rsqrt(ms + RMS_EPS) * gfin_ref[...]


def _final(x2, modf, wts, yg, g_final, tok_block_offset, tm):
    B, S, _ = x2.shape
    nt = S // tm
    tok = lambda b, i: (b, i, 0)
    mod_rows = modf.shape[1]
    mod_spec = (pl.BlockSpec((None, 1, 3 * D_MODEL), lambda b, i: (b, 0, 0)) if mod_rows == 1
                else pl.BlockSpec((None, tm, 3 * D_MODEL), tok))
    return pl.pallas_call(
        _final_kernel,
        grid=(B, nt),
        in_specs=[pl.BlockSpec((None, tm, D_MODEL), tok), mod_spec,
                  pl.BlockSpec((tm, TOP_K), lambda b, i: (tok_block_offset + b * nt + i, 0)),
                  pl.BlockSpec((TOP_K, ROW_CHUNKS, tm, CHUNK),
                               lambda b, i: (0, 0, tok_block_offset + b * nt + i, 0)),
                  _const_spec((1, D_MODEL))],
        out_specs=pl.BlockSpec((None, tm, D_MODEL), tok),
        out_shape=jax.ShapeDtypeStruct((B, S, D_MODEL), F32),
        compiler_params=_params(("arbitrary", "arbitrary")),
        name="final",
    )(x2, modf, wts, yg, g_final)


DEST_LANES = 1024


def _dest_kernel(pstart_ref, idx_ref, pos_ref, o_ref):
    idx = idx_ref[...]

    def add_start(e, acc):
        return acc + jnp.where(idx == e, pstart_ref[e], 0)

    o_ref[...] = lax.fori_loop(0, N_EXPERTS, add_start, pos_ref[...])


def _dest(pstart, idx_t, pos_t):
    n = idx_t.shape[1]
    spec = pl.BlockSpec((TOP_K, DEST_LANES), lambda i, ps: (0, i))
    return pl.pallas_call(
        _dest_kernel,
        grid_spec=pltpu.PrefetchScalarGridSpec(num_scalar_prefetch=1, grid=(n // DEST_LANES,),
                                               in_specs=[spec, spec], out_specs=spec),
        out_shape=jax.ShapeDtypeStruct((TOP_K, n), I32),
        compiler_params=_params(("arbitrary",)),
        name="dest",
    )(pstart, idx_t, pos_t)


def _sc_worker_id():
    return lax.axis_index("s") * 2 + lax.axis_index("c")


def _dispatch_rows(h_all, dest_blocks, rows_out):
    nsteps = dest_blocks.shape[0]
    mesh = plsc.VectorSubcoreMesh(core_axis_name="c", subcore_axis_name="s")

    @functools.partial(
        pl.kernel, mesh=mesh,
        out_type=jax.ShapeDtypeStruct((ROW_CHUNKS, rows_out, CHUNK), F32),
        scratch_types=[pltpu.VMEM((TOP_K, SC_WINDOW), I32),
                       pltpu.VMEM((SC_WINDOW, CHUNK), F32),
                       pltpu.SemaphoreType.DMA],
    )
    def k(x_hbm, d_hbm, o_hbm, idx_v, rows_v, sem):
        @pl.loop(_sc_worker_id(), nsteps, step=SC_WORKERS)
        def _(s):
            base = pl.multiple_of(s * SC_WINDOW, SC_WINDOW)
            pltpu.sync_copy(d_hbm.at[s], idx_v)
            for c in range(ROW_CHUNKS):
                pltpu.sync_copy(x_hbm.at[c, pl.ds(base, SC_WINDOW)], rows_v)
                for kk in range(TOP_K):
                    pltpu.async_copy(rows_v, o_hbm.at[c].at[idx_v.at[kk]], sem).wait()

    return k(h_all, dest_blocks)


def _gather_rows(y_sorted, dest_blocks):
    nsteps = dest_blocks.shape[0]
    ntok = nsteps * SC_WINDOW
    mesh = plsc.VectorSubcoreMesh(core_axis_name="c", subcore_axis_name="s")

    @functools.partial(
        pl.kernel, mesh=mesh,
        out_type=jax.ShapeDtypeStruct((TOP_K, ROW_CHUNKS, ntok, CHUNK), F32),
        scratch_types=[pltpu.VMEM((TOP_K, SC_WINDOW), I32),
                       pltpu.VMEM((SC_WINDOW, CHUNK), F32),
                       pltpu.SemaphoreType.DMA],
    )
    def k(y_hbm, d_hbm, o_hbm, idx_v, rows_v, sem):
        @pl.loop(_sc_worker_id(), nsteps, step=SC_WORKERS)
        def _(s):
            base = pl.multiple_of(s * SC_WINDOW, SC_WINDOW)
            pltpu.sync_copy(d_hbm.at[s], idx_v)
            for c in range(ROW_CHUNKS):
                for kk in range(TOP_K):
                    pltpu.async_copy(y_hbm.at[c].at[idx_v.at[kk]], rows_v, sem).wait()
                    pltpu.sync_copy(rows_v, o_hbm.at[kk, c, pl.ds(base, SC_WINDOW)])

    return k(y_sorted, dest_blocks)


PAST_LEN = 8192


def _rope_tables(pos):
    half = HEAD_DIM // 2
    inv_freq = ROPE_THETA ** (-jnp.arange(half, dtype=F32) / half)
    ang = pos.astype(F32)[:, None] * inv_freq[None, :]
    cos, sin = jnp.cos(ang), jnp.sin(ang)
    reps = LANES // HEAD_DIM
    return jnp.tile(cos, (1, 2 * reps)), jnp.tile(jnp.concatenate([-sin, sin], axis=1), (1, reps))


def _rope_parts(seq, tm):
    half = HEAD_DIM // 2
    inv_freq = ROPE_THETA ** (-jnp.arange(half, dtype=F32) / half)
    inv_l = jnp.tile(inv_freq, LANES // half)
    start = (jnp.arange(seq // tm, dtype=I32) * tm).astype(F32)
    ang_b = start[:, None] * inv_l[None, :]
    parts = [jnp.stack([jnp.cos(ang_b), jnp.sin(ang_b)], axis=1)]
    r = jnp.arange(tm, dtype=I32)
    for d in DILATIONS:
        n = tm // d
        offset = ((r % n) * d + r // n).astype(F32)
        ang = offset[:, None] * inv_l[None, :]
        parts += [jnp.cos(ang), jnp.sin(ang)]
    return parts


def _kv_tail(kg, vg, keep):
    b, d, l, _ = kg.shape
    n = keep // d

    def natural(t):
        t = t[:, :, l - n:, :].transpose(0, 2, 1, 3)
        return t.reshape(b, keep, HEADS, HEAD_DIM)

    return jnp.stack([natural(kg), natural(vg)], axis=2).astype(F32)[None]


def kernel(x_prompt, x_sample, cache_kv_w128, cache_kv_w512, cache_kv_w2048, state_conv, c_prompt, c_sample,
           g_mix, w_ada_mix, b_ada_mix, w_in, w_dw, b_dw, ln_conv_g, ln_conv_b, w_conv_out, w_att_out, w_o,
           g_ffn, w_ada_ffn, b_ada_ffn, w_router, b_router, w_exp_gate, w_exp_up, w_exp_down,
           w_sh_gate, w_sh_up, w_sh_down, g_final):
    B, S, _ = x_prompt.shape
    ns, T, _ = x_sample.shape
    assert g_mix.shape[0] == 1 and T == 1
    tm = min(TOKEN_TILE, S)
    span = DILATIONS[2] * 16
    assert S % tm == 0 and tm % span == 0 and S % (DILATIONS[2] * Q_BLOCK) == 0
    assert ns % SC_WINDOW == 0 and (B * S) % ns == 0
    caches = (cache_kv_w128, cache_kv_w512, cache_kv_w2048)

    row = lambda v: v.reshape(1, -1)
    w_in_b = w_in[0].astype(BF16)
    w_co_b = w_conv_out[0].astype(BF16)
    w_ao_b = w_att_out[0].astype(BF16)
    w_o_b = w_o[0].astype(BF16)
    wsg_b = w_sh_gate[0].astype(BF16)
    wsu_b = w_sh_up[0].astype(BF16)
    wsd_b = w_sh_down[0].astype(BF16)

    n_c = B + ns
    c_all = jnp.concatenate([c_prompt, c_sample], axis=0)
    c_all = jnp.pad(c_all, ((0, -n_c % 8), (0, 0)))
    mod_mix = _ada(c_all, w_ada_mix[0], b_ada_mix[0])
    mod_ffn = _ada(c_all, w_ada_ffn[0], b_ada_ffn[0])
    modm_p = mod_mix[:B].reshape(B, 1, 3 * D_MODEL)
    modf_p = mod_ffn[:B].reshape(B, 1, 3 * D_MODEL)
    modm_s = mod_mix[B:n_c].reshape(1, ns, 3 * D_MODEL)
    modf_s = mod_ffn[B:n_c].reshape(1, ns, 3 * D_MODEL)

    tables = _rope_parts(S, tm)
    (q0, k0, v0, q1, k1, v1, q2, k2, v2, gc_p, ga_p, utail) = _inproj_prompt(
        x_prompt, modm_p, row(g_mix[0]), w_in_b, tables, w_dw[0], row(b_dw[0]),
        row(ln_conv_g[0]), row(ln_conv_b[0]), w_co_b, tm)
    conv_prompt = utail[:, CONV_HALO - (CONV_WIDTH - 1):][None]

    attn_in = []
    lse_in = []
    kv_prompt = []
    for (qg, kg, vg), d in zip(((q0, k0, v0), (q1, k1, v1), (q2, k2, v2)), DILATIONS):
        l = S // d
        flat = lambda t: t.reshape(B * d, l, D_SLOT)
        o, lse = _attn_prompt(flat(qg), flat(kg), flat(vg))
        if d == 1:
            attn_in.append(o.reshape(B, S, D_SLOT))
            lse_in.append(lse.reshape(B, S, LANES))
        else:
            attn_in.append(o.reshape(B, d, l, D_SLOT))
            lse_in.append(lse.reshape(B, d, l, LANES))
        keep = min(Q_BLOCK * d, S)
        kv_prompt.append(_kv_tail(kg.reshape(B, d, l, D_SLOT), vg.reshape(B, d, l, D_SLOT), keep))

    wr_t = w_router[0].T
    wr_hi = wr_t.astype(BF16)
    wr_lo = (wr_t - wr_hi.astype(F32)).astype(BF16)
    b_col = b_router[0].reshape(N_EXPERTS, 1)
    zero_carry = jnp.zeros((N_EXPERTS, 1), F32)
    h2_p, x2_p, idx_p, wts_p, pos_p, cnt_p = _post(
        x_prompt, modm_p, (*attn_in, *lse_in), gc_p, ga_p, w_ao_b, w_o_b, row(g_ffn[0]), modf_p,
        wr_hi, wr_lo, b_col, wsg_b, wsu_b, wsd_b, zero_carry, tm, True)

    cos_s, sin_s = _rope_tables(jnp.full((1,), PAST_LEN, I32))
    u_s, q_s, k_s, v_s, sgc_s, ga_s = _inproj_sample(
        x_sample.reshape(ns, D_MODEL), mod_mix[B:n_c], row(g_mix[0]), w_in_b, cos_s, sin_s)
    heads = lambda t: t.reshape(ns, N_GROUPS, 1, HEADS, HEAD_DIM)
    qkv_rows = jnp.concatenate([heads(q_s), heads(k_s), heads(v_s)], axis=2)
    kv_sample, outs_s, lses_s = [], [], []
    for gi, (cache, d) in enumerate(zip(caches, DILATIONS)):
        cache_t = jnp.transpose(cache[0], (0, 2, 3, 4, 1))
        new_t, o, lse = _sample_cache(cache_t, qkv_rows, gi, d)
        kv_sample.append(jnp.transpose(new_t, (0, 4, 1, 2, 3))[None])
        outs_s.append(o.reshape(ns, D_SLOT))
        lses_s.append(lse.reshape(ns, D_SLOT))
    state_t = jnp.transpose(state_conv[0], (1, 0, 2))
    gc_s, att_s, cst_t = _sample_mix(state_t, u_s, sgc_s, w_dw[0], row(b_dw[0]), row(ln_conv_g[0]),
                                     row(ln_conv_b[0]), w_co_b, outs_s, lses_s)
    conv_sample = jnp.transpose(cst_t, (1, 0, 2))[None]
    as3 = lambda t: t.reshape(1, ns, t.shape[-1])
    h2_s, x2_s, idx_s, wts_s, pos_s, cnt = _post(
        as3(x_sample.reshape(ns, D_MODEL)), modm_s, (as3(att_s),), as3(gc_s), as3(ga_s), w_ao_b, w_o_b,
        row(g_ffn[0]), modf_s, wr_hi, wr_lo, b_col, wsg_b, wsu_b, wsd_b, cnt_p, ns, False)

    bm = EXPERT_BLOCK
    ntok = B * S + ns
    nblk = (ntok * TOP_K + N_EXPERTS * (bm - 1)) // bm
    counts = cnt[:, 0].astype(I32)
    padded = (counts + bm - 1) // bm * bm
    pend = jnp.cumsum(padded)
    pstart = pend - padded
    lane_pad = ((0, 0), (0, -ntok % DEST_LANES))
    idx_all = jnp.pad(jnp.concatenate([idx_p, idx_s], axis=1), lane_pad)
    pos_all = jnp.pad(jnp.concatenate([pos_p, pos_s], axis=1), lane_pad)
    dest = _dest(pstart, idx_all, pos_all)[:, :ntok]
    dest_blocks = dest.reshape(TOP_K, ntok // SC_WINDOW, SC_WINDOW).transpose(1, 0, 2)
    wts_tok = jnp.concatenate([wts_p, wts_s], axis=1).T
    blk_row0 = jnp.arange(nblk, dtype=I32) * bm
    blk_e = jnp.minimum(jnp.sum((pend[None, :] <= blk_row0[:, None]).astype(I32), axis=1), N_EXPERTS - 1)
    mine = blk_e[:, None] == jnp.arange(N_EXPERTS, dtype=I32)[None, :]
    cnt_b = jnp.sum(jnp.where(mine, counts[None, :], 0), axis=1)
    start_b = jnp.sum(jnp.where(mine, pstart[None, :], 0), axis=1)
    blk_valid = jnp.clip(cnt_b - (blk_row0 - start_b), 0, bm).astype(I32)
    n_used = (pend[N_EXPERTS - 1] // bm).reshape(1)

    h_all = jnp.concatenate([h2_p, h2_s], axis=1)
    x_sorted = _dispatch_rows(h_all, dest_blocks, nblk * bm)
    y_sorted = _experts(blk_e, blk_valid, n_used, x_sorted, w_exp_gate[0], w_exp_up[0], w_exp_down[0], bm)
    yg = _gather_rows(y_sorted, dest_blocks)

    y_prompt = _final(x2_p, modf_p, wts_tok, yg, row(g_final), 0, tm)
    y_sample = _final(x2_s, modf_s, wts_tok, yg, row(g_final), (B * S) // ns, ns).reshape(ns, 1, D_MODEL)

    return (y_prompt, y_sample, kv_prompt[0], kv_prompt[1], kv_prompt[2], conv_prompt,
            kv_sample[0], kv_sample[1], kv_sample[2], conv_sample)
```

```python
import functools

import jax
import jax.numpy as jnp
from jax import lax
from jax.experimental import pallas as pl
from jax.experimental.pallas import tpu as pltpu
from jax.experimental.pallas import tpu_sc as plsc

F32 = jnp.float32
BF16 = jnp.bfloat16
I32 = jnp.int32

D_MODEL = 1024
D_CONV = 1024
CONV_WIDTH = 31
HEAD_DIM = 64
HEADS = 8
D_SLOT = HEADS * HEAD_DIM
DILATIONS = (1, 4, 16)
N_GROUPS = 3
D_ATT = N_GROUPS * D_SLOT
Q_BLOCK = 128
ROPE_THETA = 10000.0
N_EXPERTS = 256
TOP_K = 8
N_EXPERT_GROUPS = 8
TOPK_GROUPS = 4
GROUP_SIZE = N_EXPERTS // N_EXPERT_GROUPS
D_EXPERT = 256
ROUTED_SCALE = 2.5
RMS_EPS = 1e-6
LN_EPS = 1e-5

COL_A = 0
COL_B = D_CONV
COL_Q = 2 * D_CONV
COL_K = COL_Q + D_ATT
COL_V = COL_K + D_ATT
COL_GC = COL_V + D_ATT
COL_GA = COL_GC + D_MODEL
D_IN = COL_GA + D_MODEL

LANES = 128
CONV_HALO = 32
TOKEN_TILE = 512
EXPERT_BLOCK = 1152
EXPERT_ROW_STEPS = (128, 1024, 1152)
ROW_CHUNKS = 2
CHUNK = D_MODEL // 2 // ROW_CHUNKS
SC_WORKERS = 32
SC_WINDOW = 128
NEG_BIG = -1e30
VMEM_LIMIT = 56 * 1024 * 1024


def _sigmoid(x):
    return 1.0 / (1.0 + jnp.exp(-x))


def _const_spec(shape):
    nd = len(shape)
    return pl.BlockSpec(shape, lambda *_: (0,) * nd, pipeline_mode=pl.Buffered(1))


def _params(sem):
    return pltpu.CompilerParams(dimension_semantics=sem, vmem_limit_bytes=VMEM_LIMIT)


def _ada_kernel(c_ref, w_ref, b_ref, o_ref):
    c = c_ref[...]
    s = (c * _sigmoid(c)).astype(BF16)
    o_ref[...] = jnp.dot(s, w_ref[...].astype(BF16), preferred_element_type=F32) + b_ref[...]


def _ada(c_all, w, b):
    rows = c_all.shape[0]
    cols = w.shape[1]
    tn = 768
    return pl.pallas_call(
        _ada_kernel,
        grid=(cols // tn,),
        in_specs=[pl.BlockSpec((rows, D_MODEL), lambda j: (0, 0)),
                  pl.BlockSpec((D_MODEL, tn), lambda j: (0, j)),
                  pl.BlockSpec((1, tn), lambda j: (0, j))],
        out_specs=pl.BlockSpec((rows, tn), lambda j: (0, j)),
        out_shape=jax.ShapeDtypeStruct((rows, cols), F32),
        compiler_params=_params(("arbitrary",)),
        name="ada",
    )(c_all, w, b.reshape(1, cols))


def _modulated_norm(x, g, mod):
    shift = mod[:, 0:D_MODEL]
    scale = mod[:, D_MODEL:2 * D_MODEL]
    ms = jnp.mean(x * x, axis=-1, keepdims=True)
    return (x * lax.rsqrt(ms + RMS_EPS)) * g * (1.0 + scale) + shift


def _rope(t, cos, sin_signed):
    lane = lax.broadcasted_iota(I32, (t.shape[0], LANES), 1)
    first_half = (lane & (HEAD_DIM - 1)) < (HEAD_DIM // 2)
    outs = []
    for j in range(D_SLOT // LANES):
        ch = t[:, j * LANES:(j + 1) * LANES]
        rot = jnp.where(first_half, pltpu.roll(ch, LANES - HEAD_DIM // 2, 1), pltpu.roll(ch, HEAD_DIM // 2, 1))
        outs.append(ch * cos + rot * sin_signed)
    return jnp.concatenate(outs, axis=1)


def _layer_norm_swish(y, g, b):
    mu = jnp.mean(y, axis=-1, keepdims=True)
    yc = y - mu
    var = jnp.mean(yc * yc, axis=-1, keepdims=True)
    z = yc * lax.rsqrt(var + LN_EPS) * g + b
    return z * _sigmoid(z)


def _inproj_kernel(x_ref, mod_ref, g_ref, w_ref, base_ref, c0_ref, s0_ref, c1_ref, s1_ref, c2_ref, s2_ref,
                   wdw_ref, bdw_ref, lng_ref, lnb_ref, wco_ref,
                   q0_ref, k0_ref, v0_ref, q1_ref, k1_ref, v1_ref, q2_ref, k2_ref, v2_ref,
                   gc_ref, ga_ref, ut_ref,
                   hf_ref, hb0_ref, hb1_ref, hb2_ref, ubuf_ref, *, tm):
    i = pl.program_id(1)
    cw = 512

    h = _modulated_norm(x_ref[...], g_ref[...], mod_ref[...])
    nlc = D_MODEL // LANES
    hb0_ref[...] = h.astype(BF16)

    def mm(hb, col, width=cw):
        return jnp.dot(hb[...], w_ref[:, col:col + width], preferred_element_type=F32)

    @pl.when(i == 0)
    def _():
        ubuf_ref[:, 0:CONV_HALO, :] = jnp.zeros((nlc, CONV_HALO, LANES), F32)

    for c in range(0, D_CONV, cw):
        a = mm(hb0_ref, COL_A + c)
        b = mm(hb0_ref, COL_B + c)
        u = a * _sigmoid(b)
        for cc in range(cw // LANES):
            ubuf_ref[c // LANES + cc, CONV_HALO:CONV_HALO + tm, :] = u[:, cc * LANES:(cc + 1) * LANES]

    for c in range(nlc):
        hf_ref[c] = h[:, c * LANES:(c + 1) * LANES]
    for d, hb in ((DILATIONS[1], hb1_ref), (DILATIONS[2], hb2_ref)):
        n = tm // d
        for r in range(d):
            for c in range(nlc):
                hb[r * n:(r + 1) * n, c * LANES:(c + 1) * LANES] = hf_ref[c, pl.ds(r, n, stride=d), :].astype(BF16)

    groups = ((hb0_ref, 1, c0_ref, s0_ref, q0_ref, k0_ref, v0_ref),
              (hb1_ref, DILATIONS[1], c1_ref, s1_ref, q1_ref, k1_ref, v1_ref),
              (hb2_ref, DILATIONS[2], c2_ref, s2_ref, q2_ref, k2_ref, v2_ref))
    cos_b = base_ref[0:1, :]
    sin_b = base_ref[1:2, :]
    lane = lax.broadcasted_iota(I32, (1, LANES), 1)
    sign = jnp.where((lane & (HEAD_DIM - 1)) < (HEAD_DIM // 2), -1.0, 1.0)
    for gi, (hb, d, c_ref, s_ref, q_ref, k_ref, v_ref) in enumerate(groups):
        cos_w = c_ref[...]
        sin_w = s_ref[...]
        cos = cos_b * cos_w - sin_b * sin_w
        sin = (sin_b * cos_w + cos_b * sin_w) * sign
        n = tm // d

        def put(ref, val):
            vb = val.astype(BF16)
            if d == 1:
                ref[...] = vb
            else:
                for r in range(d):
                    ref[r] = vb[r * n:(r + 1) * n, :]

        put(q_ref, _rope(mm(hb, COL_Q + gi * D_SLOT), cos, sin) * (HEAD_DIM ** -0.5))
        put(k_ref, _rope(mm(hb, COL_K + gi * D_SLOT), cos, sin))
        put(v_ref, mm(hb, COL_V + gi * D_SLOT))

    rb = 32
    first_tap = CONV_HALO - (CONV_WIDTH - 1)

    def conv_rows(r, carry):
        r0 = pl.multiple_of(r * rb, rb)
        for c in range(nlc):
            cs = slice(c * LANES, (c + 1) * LANES)
            acc = jnp.broadcast_to(bdw_ref[:, cs], (rb, LANES))
            for k in range(CONV_WIDTH):
                acc = acc + wdw_ref[k:k + 1, cs] * ubuf_ref[c, pl.ds(r0 + (first_tap + k), rb), :]
            hf_ref[c, pl.ds(r0, rb), :] = acc
        return carry

    lax.fori_loop(0, tm // rb, conv_rows, 0)

    for c in range(0, D_MODEL, cw):
        ga_ref[:, c:c + cw] = _sigmoid(mm(hb0_ref, COL_GA + c)).astype(BF16)
    y_dw = jnp.concatenate([hf_ref[c] for c in range(nlc)], axis=1)
    z = _layer_norm_swish(y_dw, lng_ref[...], lnb_ref[...]).astype(BF16)
    for c in range(0, D_MODEL, cw):
        co = jnp.dot(z, wco_ref[:, c:c + cw], preferred_element_type=F32)
        gc_ref[:, c:c + cw] = (_sigmoid(mm(hb0_ref, COL_GC + c)) * co).astype(BF16)

    for c in range(nlc):
        tail = ubuf_ref[c, tm:tm + CONV_HALO, :]
        ut_ref[:, c * LANES:(c + 1) * LANES] = tail
        ubuf_ref[c, 0:CONV_HALO, :] = tail


def _inproj_prompt(x, mod, g, w_in_b, tables, w_dw, b_dw, ln_g, ln_b, w_co_b, tm):
    B, S, _ = x.shape
    nt = S // tm
    d1, d2 = DILATIONS[1], DILATIONS[2]
    tok = lambda b, i: (b, i, 0)
    tab = _const_spec((tm, LANES))
    in_specs = [
        pl.BlockSpec((None, tm, D_MODEL), tok),
        pl.BlockSpec((None, 1, 3 * D_MODEL), lambda b, i: (b, 0, 0)),
        _const_spec((1, D_MODEL)),
        _const_spec((D_MODEL, D_IN)),
        pl.BlockSpec((None, 2, LANES), lambda b, i: (i, 0, 0)),
        tab, tab, tab, tab, tab, tab,
        _const_spec((CONV_WIDTH, D_CONV)),
        _const_spec((1, D_CONV)),
        _const_spec((1, D_CONV)),
        _const_spec((1, D_CONV)),
        _const_spec((D_CONV, D_MODEL)),
    ]
    nat = pl.BlockSpec((None, tm, D_SLOT), tok)
    st1 = pl.BlockSpec((None, d1, tm // d1, D_SLOT), lambda b, i: (b, 0, i, 0))
    st2 = pl.BlockSpec((None, d2, tm // d2, D_SLOT), lambda b, i: (b, 0, i, 0))
    out_specs = [nat, nat, nat, st1, st1, st1, st2, st2, st2,
                 pl.BlockSpec((None, tm, D_MODEL), tok),
                 pl.BlockSpec((None, tm, D_MODEL), tok),
                 pl.BlockSpec((None, CONV_HALO, D_CONV), lambda b, i: (b, 0, 0))]
    s0 = jax.ShapeDtypeStruct((B, S, D_SLOT), BF16)
    s1 = jax.ShapeDtypeStruct((B, d1, S // d1, D_SLOT), BF16)
    s2 = jax.ShapeDtypeStruct((B, d2, S // d2, D_SLOT), BF16)
    out_shape = [s0, s0, s0, s1, s1, s1, s2, s2, s2,
                 jax.ShapeDtypeStruct((B, S, D_MODEL), BF16),
                 jax.ShapeDtypeStruct((B, S, D_MODEL), BF16),
                 jax.ShapeDtypeStruct((B, CONV_HALO, D_CONV), F32)]
    scratch = [pltpu.VMEM((D_MODEL // LANES, tm, LANES), F32),
               pltpu.VMEM((tm, D_MODEL), BF16),
               pltpu.VMEM((tm, D_MODEL), BF16),
               pltpu.VMEM((tm, D_MODEL), BF16),
               pltpu.VMEM((D_CONV // LANES, tm + CONV_HALO, LANES), F32)]
    return pl.pallas_call(
        functools.partial(_inproj_kernel, tm=tm),
        grid=(B, nt),
        in_specs=in_specs, out_specs=out_specs, out_shape=out_shape,
        scratch_shapes=scratch,
        compiler_params=_params(("arbitrary", "arbitrary")),
        name="inproj",
    )(x, mod, g, w_in_b, *tables, w_dw, b_dw, ln_g, ln_b, w_co_b)


def _attn_kernel(q_ref, kp_ref, kc_ref, vp_ref, vc_ref, o_ref, lse_ref, k_all, v_all, *, nq):
    j = pl.program_id(1)
    k_all[0:Q_BLOCK, :] = kp_ref[...]
    k_all[Q_BLOCK:(nq + 1) * Q_BLOCK, :] = kc_ref[...]
    v_all[0:Q_BLOCK, :] = vp_ref[...]
    v_all[Q_BLOCK:(nq + 1) * Q_BLOCK, :] = vc_ref[...]
    qi = lax.broadcasted_iota(I32, (Q_BLOCK, 2 * Q_BLOCK), 0)
    ki = lax.broadcasted_iota(I32, (Q_BLOCK, 2 * Q_BLOCK), 1)
    back = Q_BLOCK + qi - ki
    in_band = jnp.logical_and(back >= 0, back <= Q_BLOCK)
    first_mask = jnp.logical_and(in_band, jnp.logical_or(ki >= Q_BLOCK, j > 0))
    lane = lax.broadcasted_iota(I32, (Q_BLOCK, LANES), 1)
    even = lax.broadcasted_iota(I32, (1, LANES), 1) < HEAD_DIM
    zero = jnp.zeros((), BF16)
    nt = (((1,), (1,)), ((), ()))
    for sub in range(nq):
        mask = first_mask if sub == 0 else in_band
        qrows = slice(sub * Q_BLOCK, (sub + 1) * Q_BLOCK)
        krows = slice(sub * Q_BLOCK, (sub + 2) * Q_BLOCK)
        scores = []
        for h in range(HEADS):
            pair = slice((h // 2) * LANES, (h // 2 + 1) * LANES)
            mine = even if h % 2 == 0 else jnp.logical_not(even)
            qh = jnp.where(mine, q_ref[qrows, pair], zero)
            scores.append(lax.dot_general(qh, k_all[krows, pair], nt, preferred_element_type=F32))
        probs, dens = [], []
        lse_all = jnp.zeros((Q_BLOCK, LANES), F32)
        for h in range(HEADS):
            s = jnp.where(mask, scores[h], NEG_BIG)
            m = jnp.max(s, axis=-1, keepdims=True)
            p = jnp.exp(s - m)
            den = jnp.sum(p, axis=-1, keepdims=True)
            probs.append(p.astype(BF16))
            dens.append(den)
            lse_all = jnp.where(lane == h, m + jnp.log(den), lse_all)
        for hp in range(HEADS // 2):
            pair = slice(hp * LANES, (hp + 1) * LANES)
            v2 = v_all[krows, pair]
            oe = jnp.dot(probs[2 * hp], v2, preferred_element_type=F32) / dens[2 * hp]
            oo = jnp.dot(probs[2 * hp + 1], v2, preferred_element_type=F32) / dens[2 * hp + 1]
            o_ref[qrows, pair] = jnp.where(even, oe, oo).astype(BF16)
        lse_ref[qrows, :] = lse_all


def _attn_prompt(q, k, v):
    ns, L, _ = q.shape
    nq = max(c for c in (4, 2, 1) if (L // Q_BLOCK) % c == 0)
    rows = nq * Q_BLOCK
    cur = pl.BlockSpec((None, rows, D_SLOT), lambda n, j: (n, j, 0))
    prev = pl.BlockSpec((None, Q_BLOCK, D_SLOT), lambda n, j: (n, jnp.maximum(j * nq - 1, 0), 0))
    return pl.pallas_call(
        functools.partial(_attn_kernel, nq=nq),
        grid=(ns, L // rows),
        in_specs=[cur, prev, cur, prev, cur],
        out_specs=[cur, pl.BlockSpec((None, rows, LANES), lambda n, j: (n, j, 0))],
        out_shape=[jax.ShapeDtypeStruct((ns, L, D_SLOT), BF16),
                   jax.ShapeDtypeStruct((ns, L, LANES), F32)],
        scratch_shapes=[pltpu.VMEM((rows + Q_BLOCK, D_SLOT), BF16), pltpu.VMEM((rows + Q_BLOCK, D_SLOT), BF16)],
        compiler_params=_params(("arbitrary", "arbitrary")),
        name="attn",
    )(q, k, k, v, v)


def _route_t(logits_t, b_col, carry, tm):
    ninf = -jnp.inf
    scores = _sigmoid(logits_t)
    sel = scores + b_col
    rowf = lax.broadcasted_iota(I32, (N_EXPERTS, tm), 0).astype(F32)
    past_end = float(N_EXPERTS)

    def first_max(x, rows):
        m = jnp.max(x, axis=0, keepdims=True)
        return m, jnp.min(jnp.where(x == m, rows, past_end), axis=0, keepdims=True)

    gs = []
    rowg = lax.broadcasted_iota(I32, (GROUP_SIZE, tm), 0).astype(F32)
    for g in range(N_EXPERT_GROUPS):
        rs = slice(g * GROUP_SIZE, (g + 1) * GROUP_SIZE)
        m1, i1 = first_max(sel[rs], rowg)
        m2 = jnp.max(jnp.where(rowg == i1, ninf, sel[rs]), axis=0, keepdims=True)
        gs.append(m1 + m2)
    pieces = []
    for g in range(N_EXPERT_GROUPS):
        beaten = jnp.zeros((1, tm), F32)
        for g2 in range(N_EXPERT_GROUPS):
            if g2 != g:
                better = gs[g2] >= gs[g] if g2 < g else gs[g2] > gs[g]
                beaten = beaten + jnp.where(better, 1.0, 0.0)
        rs = slice(g * GROUP_SIZE, (g + 1) * GROUP_SIZE)
        pieces.append(jnp.where(beaten < TOPK_GROUPS, sel[rs], ninf))
    selm = jnp.concatenate(pieces, axis=0)

    idx_rows, w_rows = [], []
    picked = jnp.zeros((N_EXPERTS, tm), F32)
    for _ in range(TOP_K):
        _, ik = first_max(selm, rowf)
        hit = rowf == ik
        w_rows.append(jnp.sum(jnp.where(hit, scores, 0.0), axis=0, keepdims=True))
        selm = jnp.where(hit, ninf, selm)
        picked = jnp.where(hit, 1.0, picked)
        idx_rows.append(ik)
    wsum = w_rows[0]
    for wk in w_rows[1:]:
        wsum = wsum + wk
    denom = wsum + 1e-20

    ti = lax.broadcasted_iota(I32, (tm, tm), 0)
    tj = lax.broadcasted_iota(I32, (tm, tm), 1)
    earlier = jnp.where(ti < tj, 1.0, 0.0).astype(BF16)
    before = jnp.dot(picked.astype(BF16), earlier, preferred_element_type=F32) + carry
    new_carry = carry + jnp.sum(picked, axis=1, keepdims=True)
    pos_rows = [jnp.sum(jnp.where(rowf == ik, before, 0.0), axis=0, keepdims=True) for ik in idx_rows]

    idx_o = jnp.concatenate(idx_rows, axis=0).astype(I32)
    w_o = jnp.concatenate([wk / denom * ROUTED_SCALE for wk in w_rows], axis=0)
    pos_o = jnp.concatenate(pos_rows, axis=0).astype(I32)
    return idx_o, w_o, pos_o, new_carry


def _pack_rows(xb):
    bits = lax.bitcast_convert_type(xb.astype(F32), jnp.uint32)
    half = D_MODEL // 2
    word = bits[:, half:] | (bits[:, :half] >> 16)
    return lax.bitcast_convert_type(word, F32)


def _unpack_rows(words):
    bits = lax.bitcast_convert_type(words, jnp.uint32)
    lo = lax.bitcast_convert_type(bits << 16, F32)
    hi = lax.bitcast_convert_type(bits & jnp.uint32(0xFFFF0000), F32)
    return jnp.concatenate([lo, hi], axis=1)


def _post_kernel(*refs, tm, combine):
    refs = list(refs)
    x_ref, modm_ref = refs[0:2]
    p = 2
    if combine:
        o0_ref, o1_ref, o2_ref, l0_ref, l1_ref, l2_ref = refs[p:p + 6]
        p += 6
    else:
        att_ref = refs[p]
        p += 1
    (gc_ref, ga_ref, wao_ref, wo_ref, gffn_ref, modf_ref, wrh_ref, wrl_ref, br_ref,
     wsg_ref, wsu_ref, wsd_ref, cin_ref) = refs[p:p + 13]
    p += 13
    h2_ref, x2_ref, idx_ref, wts_ref, pos_ref, cnt_ref = refs[p:p + 6]
    p += 6
    if combine:
        o1n_ref, o2n_ref, l1n_ref, l2n_ref = refs[p:p + 4]

    first = jnp.logical_and(pl.program_id(0) == 0, pl.program_id(1) == 0)

    @pl.when(first)
    def _():
        cnt_ref[...] = cin_ref[...]

    if combine:
        nsc = D_SLOT // LANES
        for d, o_ref, l_ref, on_ref, ln_ref in ((DILATIONS[1], o1_ref, l1_ref, o1n_ref, l1n_ref),
                                                (DILATIONS[2], o2_ref, l2_ref, o2n_ref, l2n_ref)):
            n = tm // d
            for r in range(d):
                ln_ref[pl.ds(r, n, stride=d), :] = l_ref[r]
                orow = o_ref[r].astype(F32)
                for c in range(nsc):
                    on_ref[c, pl.ds(r, n, stride=d), :] = orow[:, c * LANES:(c + 1) * LANES]
        l0 = l0_ref[...]
        l1 = l1n_ref[...]
        l2 = l2n_ref[...]
        mx = jnp.maximum(jnp.maximum(l0, l1), l2)
        e0 = jnp.exp(l0 - mx)
        e1 = jnp.exp(l1 - mx)
        e2 = jnp.exp(l2 - mx)
        esum = e0 + e1 + e2
        even = lax.broadcasted_iota(I32, (tm, LANES), 1) < HEAD_DIM

        def expand(w):
            wide = lambda h: jnp.broadcast_to(w[:, h:h + 1], (tm, LANES))
            return jnp.concatenate([jnp.where(even, wide(2 * c), wide(2 * c + 1)) for c in range(nsc)], axis=1)

        o1n = jnp.concatenate([o1n_ref[c] for c in range(nsc)], axis=1)
        o2n = jnp.concatenate([o2n_ref[c] for c in range(nsc)], axis=1)
        att = (expand(e0 / esum) * o0_ref[...].astype(F32) + expand(e1 / esum) * o1n
               + expand(e2 / esum) * o2n)
    else:
        att = att_ref[...]

    att_out = jnp.dot(att.astype(BF16), wao_ref[...], preferred_element_type=F32)
    merged = gc_ref[...] + ga_ref[...] * att_out.astype(BF16)
    y = jnp.dot(merged, wo_ref[...], preferred_element_type=F32)
    x1 = x_ref[...] + modm_ref[:, 2 * D_MODEL:3 * D_MODEL] * y

    modf = modf_ref[...]
    h2 = _modulated_norm(x1, gffn_ref[...], modf)
    h2b = h2.astype(BF16)
    words = _pack_rows(h2b)
    for c in range(ROW_CHUNKS):
        h2_ref[c] = words[:, c * CHUNK:(c + 1) * CHUNK]
    sg = jnp.dot(h2b, wsg_ref[...], preferred_element_type=F32)
    su = jnp.dot(h2b, wsu_ref[...], preferred_element_type=F32)
    sh = jnp.dot((sg * _sigmoid(sg) * su).astype(BF16), wsd_ref[...], preferred_element_type=F32)
    x2_ref[...] = x1 + modf[:, 2 * D_MODEL:3 * D_MODEL] * sh

    h2l = (h2 - h2b.astype(F32)).astype(BF16)
    nt = (((1,), (1,)), ((), ()))
    logits_t = (lax.dot_general(wrh_ref[...], h2b, nt, preferred_element_type=F32)
                + lax.dot_general(wrl_ref[...], h2b, nt, preferred_element_type=F32)
                + lax.dot_general(wrh_ref[...], h2l, nt, preferred_element_type=F32))
    carry = cnt_ref[...]
    for lc in range(tm // LANES):
        ls = slice(lc * LANES, (lc + 1) * LANES)
        idx_o, w_o, pos_o, carry = _route_t(logits_t[:, ls], br_ref[...], carry, LANES)
        idx_ref[:, ls] = idx_o
        wts_ref[:, ls] = w_o
        pos_ref[:, ls] = pos_o
    cnt_ref[...] = carry


def _post(x, modm, attn_inputs, gc, ga, w_ao_b, w_o_b, g_ffn, modf, wr_hi, wr_lo, b_router,
          wsg_b, wsu_b, wsd_b, carry_in, tm, combine):
    B, S, _ = x.shape
    nt = S // tm
    tok = lambda b, i: (b, i, 0)
    mod_rows = modm.shape[1]
    mod_spec = (pl.BlockSpec((None, 1, 3 * D_MODEL), lambda b, i: (b, 0, 0)) if mod_rows == 1
                else pl.BlockSpec((None, tm, 3 * D_MODEL), tok))
    full = pl.BlockSpec((None, tm, D_MODEL), tok)
    in_specs = [full, mod_spec]
    scratch = []
    if combine:
        d1, d2 = DILATIONS[1], DILATIONS[2]
        in_specs += [pl.BlockSpec((None, tm, D_SLOT), tok),
                     pl.BlockSpec((None, d1, tm // d1, D_SLOT), lambda b, i: (b, 0, i, 0)),
                     pl.BlockSpec((None, d2, tm // d2, D_SLOT), lambda b, i: (b, 0, i, 0)),
                     pl.BlockSpec((None, tm, LANES), tok),
                     pl.BlockSpec((None, d1, tm // d1, LANES), lambda b, i: (b, 0, i, 0)),
                     pl.BlockSpec((None, d2, tm // d2, LANES), lambda b, i: (b, 0, i, 0))]
        scratch = [pltpu.VMEM((D_SLOT // LANES, tm, LANES), F32),
                   pltpu.VMEM((D_SLOT // LANES, tm, LANES), F32),
                   pltpu.VMEM((tm, LANES), F32),
                   pltpu.VMEM((tm, LANES), F32)]
    else:
        in_specs += [pl.BlockSpec((None, tm, D_SLOT), tok)]
    in_specs += [full, full,
                 _const_spec((D_SLOT, D_MODEL)), _const_spec((D_MODEL, D_MODEL)),
                 _const_spec((1, D_MODEL)), mod_spec,
                 _const_spec((N_EXPERTS, D_MODEL)), _const_spec((N_EXPERTS, D_MODEL)),
                 _const_spec((N_EXPERTS, 1)),
                 _const_spec((D_MODEL, D_EXPERT)), _const_spec((D_MODEL, D_EXPERT)),
                 _const_spec((D_EXPERT, D_MODEL)), _const_spec((N_EXPERTS, 1))]
    pack = pl.BlockSpec((TOP_K, tm), lambda b, i: (0, b * nt + i))
    chunked = pl.BlockSpec((ROW_CHUNKS, tm, CHUNK), lambda b, i: (0, b * nt + i, 0))
    out_specs = [chunked, full, pack, pack, pack, pl.BlockSpec((N_EXPERTS, 1), lambda b, i: (0, 0))]
    out_shape = [jax.ShapeDtypeStruct((ROW_CHUNKS, B * S, CHUNK), F32),
                 jax.ShapeDtypeStruct((B, S, D_MODEL), F32),
                 jax.ShapeDtypeStruct((TOP_K, B * S), I32),
                 jax.ShapeDtypeStruct((TOP_K, B * S), F32),
                 jax.ShapeDtypeStruct((TOP_K, B * S), I32),
                 jax.ShapeDtypeStruct((N_EXPERTS, 1), F32)]
    return pl.pallas_call(
        functools.partial(_post_kernel, tm=tm, combine=combine),
        grid=(B, nt),
        in_specs=in_specs, out_specs=out_specs, out_shape=out_shape,
        scratch_shapes=scratch,
        compiler_params=_params(("arbitrary", "arbitrary")),
        name="post_prompt" if combine else "post_sample",
    )(x, modm, *attn_inputs, gc, ga, w_ao_b, w_o_b, g_ffn, modf, wr_hi, wr_lo, b_router,
      wsg_b, wsu_b, wsd_b, carry_in)


def _inproj_sample_kernel(x_ref, mod_ref, g_ref, w_ref, cos_ref, sin_ref,
                          u_ref, q_ref, k_ref, v_ref, sgc_ref, ga_ref):
    cw = 512
    hb = _modulated_norm(x_ref[...], g_ref[...], mod_ref[...]).astype(BF16)

    def mm(col, width=cw):
        return jnp.dot(hb, w_ref[:, col:col + width], preferred_element_type=F32)

    cos = cos_ref[...]
    sin = sin_ref[...]
    for c in range(0, D_CONV, cw):
        u_ref[:, c:c + cw] = mm(COL_A + c) * _sigmoid(mm(COL_B + c))
        sgc_ref[:, c:c + cw] = _sigmoid(mm(COL_GC + c))
        ga_ref[:, c:c + cw] = _sigmoid(mm(COL_GA + c)).astype(BF16)
    for gi in range(N_GROUPS):
        cs = slice(gi * D_SLOT, (gi + 1) * D_SLOT)
        q_ref[:, cs] = _rope(mm(COL_Q + gi * D_SLOT), cos, sin) * (HEAD_DIM ** -0.5)
        k_ref[:, cs] = _rope(mm(COL_K + gi * D_SLOT), cos, sin)
        v_ref[:, cs] = mm(COL_V + gi * D_SLOT)


def _inproj_sample(x, mod, g, w_in_b, cos, sin):
    ns = x.shape[0]
    whole = lambda shape: pl.BlockSpec(shape, lambda i: (0,) * len(shape))
    f = lambda cols, dt=F32: jax.ShapeDtypeStruct((ns, cols), dt)
    return pl.pallas_call(
        _inproj_sample_kernel,
        grid=(1,),
        in_specs=[whole((ns, D_MODEL)), whole((ns, 3 * D_MODEL)), _const_spec((1, D_MODEL)),
                  _const_spec((D_MODEL, D_IN)), whole((1, LANES)), whole((1, LANES))],
        out_specs=[whole((ns, D_CONV)), whole((ns, D_ATT)), whole((ns, D_ATT)), whole((ns, D_ATT)),
                   whole((ns, D_MODEL)), whole((ns, D_MODEL))],
        out_shape=[f(D_CONV), f(D_ATT), f(D_ATT), f(D_ATT), f(D_MODEL), f(D_MODEL, BF16)],
        compiler_params=_params(("arbitrary",)),
        name="inproj_sample",
    )(x, mod, g, w_in_b, cos, sin)


def _sample_mix_kernel(state_ref, u_ref, sgc_ref, wdw_ref, bdw_ref, lng_ref, lnb_ref, wco_ref,
                       o0_ref, o1_ref, o2_ref, l0_ref, l1_ref, l2_ref,
                       gc_ref, att_ref, cst_ref):
    nprev = CONV_WIDTH - 1
    u = u_ref[...]
    y = wdw_ref[nprev:CONV_WIDTH, :] * u + bdw_ref[...]
    for k in range(nprev):
        y = y + wdw_ref[k:k + 1, :] * state_ref[k]
    z = _layer_norm_swish(y, lng_ref[...], lnb_ref[...]).astype(BF16)
    co = jnp.dot(z, wco_ref[...], preferred_element_type=F32)
    gc_ref[...] = (sgc_ref[...] * co).astype(BF16)
    for k in range(nprev - 1):
        cst_ref[k] = state_ref[k + 1]
    cst_ref[nprev - 1] = u

    l0 = l0_ref[...]
    l1 = l1_ref[...]
    l2 = l2_ref[...]
    mx = jnp.maximum(jnp.maximum(l0, l1), l2)
    e0 = jnp.exp(l0 - mx)
    e1 = jnp.exp(l1 - mx)
    e2 = jnp.exp(l2 - mx)
    esum = e0 + e1 + e2
    att_ref[...] = (e0 / esum) * o0_ref[...] + (e1 / esum) * o1_ref[...] + (e2 / esum) * o2_ref[...]


def _sample_mix(state, u, sgc, w_dw, b_dw, ln_g, ln_b, w_co_b, outs, lses):
    ns = state.shape[1]
    nprev = CONV_WIDTH - 1
    whole = lambda *shape: pl.BlockSpec(shape, lambda i: (0,) * len(shape))
    slot = whole(ns, D_SLOT)
    return pl.pallas_call(
        _sample_mix_kernel,
        grid=(1,),
        in_specs=[whole(nprev, ns, D_CONV), whole(ns, D_CONV), whole(ns, D_MODEL),
                  whole(CONV_WIDTH, D_CONV), whole(1, D_CONV), whole(1, D_CONV), whole(1, D_CONV),
                  whole(D_CONV, D_MODEL), slot, slot, slot, slot, slot, slot],
        out_specs=[whole(ns, D_MODEL), slot, whole(nprev, ns, D_CONV)],
        out_shape=[jax.ShapeDtypeStruct((ns, D_MODEL), BF16),
                   jax.ShapeDtypeStruct((ns, D_SLOT), F32),
                   jax.ShapeDtypeStruct((nprev, ns, D_CONV), F32)],
        compiler_params=_params(("arbitrary",)),
        name="sample_mix",
    )(state, u, sgc, w_dw, b_dw, ln_g, ln_b, w_co_b, *outs, *lses)


def _split3(x):
    p1 = x.astype(BF16)
    r1 = x - p1.astype(F32)
    p2 = r1.astype(BF16)
    p3 = (r1 - p2.astype(F32)).astype(BF16)
    return p1, p2, p3


def _sample_cache_kernel(c_ref, r_ref, co_ref, o_ref, l_ref, *, n, dil, sb):
    lane = lax.broadcasted_iota(I32, (1, n), 1)
    back = n - lane
    use = jnp.logical_and(jnp.logical_and(back % dil == 0, back <= Q_BLOCK * dil), back >= dil)
    last = lax.broadcasted_iota(I32, (HEAD_DIM, n), 1) == n - 1
    nt = (((1,), (1,)), ((), ()))
    eye_d = (lax.broadcasted_iota(I32, (HEAD_DIM, HEAD_DIM), 0)
             == lax.broadcasted_iota(I32, (HEAD_DIM, HEAD_DIM), 1)).astype(BF16)
    kv_rows = r_ref[:, 1:3].reshape(sb * 2 * HEADS, HEAD_DIM)
    kv_cols = sum(lax.dot_general(eye_d, part, nt, preferred_element_type=F32) for part in _split3(kv_rows))

    for b in range(sb):
        q = r_ref[b, 0]
        qb = q.astype(BF16)
        rows = []
        for h in range(HEADS):
            kh = c_ref[b, 0, h].astype(BF16)
            rows.append(jnp.dot(qb, kh, preferred_element_type=F32)[h:h + 1, :])
        s = jnp.where(use, jnp.concatenate(rows, axis=0), NEG_BIG)
        sn = jnp.sum(q * r_ref[b, 1], axis=1, keepdims=True)
        m = jnp.maximum(jnp.max(s, axis=1, keepdims=True), sn)
        p = jnp.exp(s - m)
        pn = jnp.exp(sn - m)
        den = jnp.sum(p, axis=1, keepdims=True) + pn
        pb = p.astype(BF16)
        outs = []
        for h in range(HEADS):
            vh = c_ref[b, 1, h].astype(BF16)
            outs.append(lax.dot_general(pb, vh, nt, preferred_element_type=F32)[h:h + 1, :])
        o_ref[b] = (jnp.concatenate(outs, axis=0) + pn * r_ref[b, 2]) / den
        l_ref[b] = jnp.broadcast_to(m + jnp.log(den), (HEADS, HEAD_DIM))
        for h in range(HEADS):
            for kind in range(2):
                j = (b * 2 + kind) * HEADS + h
                co_ref[b, kind, h] = jnp.where(last, kv_cols[:, j:j + 1], pltpu.roll(c_ref[b, kind, h], n - 1, 1))


def _sample_cache(cache_t, qkv_rows, gi, dil):
    ns, _, _, _, n = cache_t.shape
    seq_bytes = 2 * HEADS * HEAD_DIM * n * 4
    sb = max(1, min(4, (4 * 1024 * 1024) // seq_bytes))
    assert ns % sb == 0
    cspec = pl.BlockSpec((sb, 2, HEADS, HEAD_DIM, n), lambda b: (b, 0, 0, 0, 0))
    rspec = pl.BlockSpec((sb, None, 3, HEADS, HEAD_DIM), lambda b: (b, gi, 0, 0, 0))
    ospec = pl.BlockSpec((sb, HEADS, HEAD_DIM), lambda b: (b, 0, 0))
    slot = jax.ShapeDtypeStruct((ns, HEADS, HEAD_DIM), F32)
    return pl.pallas_call(
        functools.partial(_sample_cache_kernel, n=n, dil=dil, sb=sb),
        grid=(ns // sb,),
        in_specs=[cspec, rspec],
        out_specs=[cspec, ospec, ospec],
        out_shape=[jax.ShapeDtypeStruct(cache_t.shape, F32), slot, slot],
        compiler_params=_params(("arbitrary",)),
        name="sample_cache",
    )(cache_t, qkv_rows)


def _experts_kernel(be_ref, bv_ref, nu_ref, x_ref, wg_ref, wu_ref, wd_ref, y_ref, wgb_ref, wub_ref, wdb_ref, *, bm):
    i = pl.program_id(0)
    e = be_ref[i]
    prev = be_ref[jnp.maximum(i - 1, 0)]
    valid = jnp.where(i < nu_ref[0], bv_ref[i], 0)

    @pl.when(jnp.logical_and(valid > 0, jnp.logical_or(i == 0, e != prev)))
    def _():
        wgb_ref[...] = wg_ref[...].astype(BF16)
        wub_ref[...] = wu_ref[...].astype(BF16)
        wdb_ref[...] = wd_ref[...].astype(BF16)

    def run(rows):
        words = jnp.concatenate([x_ref[c, 0:rows, :] for c in range(ROW_CHUNKS)], axis=1)
        row = lax.broadcasted_iota(I32, (rows, D_MODEL), 0)
        x = jnp.where(row < valid, _unpack_rows(words), 0.0).astype(BF16)
        g = jnp.dot(x, wgb_ref[...], preferred_element_type=F32)
        u = jnp.dot(x, wub_ref[...], preferred_element_type=F32)
        a = (g * _sigmoid(g) * u).astype(BF16)
        y = jnp.dot(a, wdb_ref[...], preferred_element_type=F32)
        yw = _pack_rows(y.astype(BF16))
        for c in range(ROW_CHUNKS):
            y_ref[c, 0:rows, :] = yw[:, c * CHUNK:(c + 1) * CHUNK]
            if rows < bm:
                y_ref[c, rows:bm, :] = jnp.zeros((bm - rows, CHUNK), F32)

    lower = 0
    for rows in EXPERT_ROW_STEPS:
        upper_ok = valid <= rows if rows < bm else True

        @pl.when(jnp.logical_and(valid > lower, upper_ok))
        def _(rows=rows):
            run(rows)

        lower = rows


def _experts(blk_e, blk_valid, n_used, x_sorted, w_g, w_u, w_d, bm):
    rows = x_sorted.shape[1]
    nblk = rows // bm
    blk = lambda i, be, bv, nu: (0, jnp.minimum(i, nu[0] - 1), 0)
    wsel = lambda i, be, bv, nu: (be[jnp.minimum(i, nu[0] - 1)], 0, 0)
    grid_spec = pltpu.PrefetchScalarGridSpec(
        num_scalar_prefetch=3,
        grid=(nblk,),
        in_specs=[pl.BlockSpec((ROW_CHUNKS, bm, CHUNK), blk),
                  pl.BlockSpec((None, D_MODEL, D_EXPERT), wsel),
                  pl.BlockSpec((None, D_MODEL, D_EXPERT), wsel),
                  pl.BlockSpec((None, D_EXPERT, D_MODEL), wsel)],
        out_specs=pl.BlockSpec((ROW_CHUNKS, bm, CHUNK), blk),
        scratch_shapes=[pltpu.VMEM((D_MODEL, D_EXPERT), BF16), pltpu.VMEM((D_MODEL, D_EXPERT), BF16),
                        pltpu.VMEM((D_EXPERT, D_MODEL), BF16)])
    return pl.pallas_call(
        functools.partial(_experts_kernel, bm=bm),
        grid_spec=grid_spec,
        out_shape=jax.ShapeDtypeStruct((ROW_CHUNKS, rows, CHUNK), F32),
        compiler_params=_params(("arbitrary",)),
        name="experts",
    )(blk_e, blk_valid, n_used, x_sorted, w_g, w_u, w_d)


def _final_kernel(x2_ref, modf_ref, wts_ref, yg_ref, gfin_ref, o_ref):
    w = wts_ref[...]

    def picked(k):
        return _unpack_rows(jnp.concatenate([yg_ref[k, c] for c in range(ROW_CHUNKS)], axis=1))

    routed = w[:, 0:1] * picked(0)
    for k in range(1, TOP_K):
        routed = routed + w[:, k:k + 1] * picked(k)
    x = x2_ref[...] + modf_ref[:, 2 * D_MODEL:3 * D_MODEL] * routed
    ms = jnp.mean(x * x, axis=-1, keepdims=True)
    o_ref[...] = x * lax.rsqrt(ms + RMS_EPS) * gfin_ref[...]


def _final(x2, modf, wts, yg, g_final, tok_block_offset, tm):
    B, S, _ = x2.shape
    nt = S // tm
    tok = lambda b, i: (b, i, 0)
    mod_rows = modf.shape[1]
    mod_spec = (pl.BlockSpec((None, 1, 3 * D_MODEL), lambda b, i: (b, 0, 0)) if mod_rows == 1
                else pl.BlockSpec((None, tm, 3 * D_MODEL), tok))
    return pl.pallas_call(
        _final_kernel,
        grid=(B, nt),
        in_specs=[pl.BlockSpec((None, tm, D_MODEL), tok), mod_spec,
                  pl.BlockSpec((tm, TOP_K), lambda b, i: (tok_block_offset + b * nt + i, 0)),
                  pl.BlockSpec((TOP_K, ROW_CHUNKS, tm, CHUNK),
                               lambda b, i: (0, 0, tok_block_offset + b * nt + i, 0)),
                  _const_spec((1, D_MODEL))],
        out_specs=pl.BlockSpec((None, tm, D_MODEL), tok),
        out_shape=jax.ShapeDtypeStruct((B, S, D_MODEL), F32),
        compiler_params=_params(("arbitrary", "arbitrary")),
        name="final",
    )(x2, modf, wts, yg, g_final)


DEST_LANES = 1024


def _dest_kernel(pstart_ref, idx_ref, pos_ref, o_ref):
    idx = idx_ref[...]

    def add_start(e, acc):
        return acc + jnp.where(idx == e, pstart_ref[e], 0)

    o_ref[...] = lax.fori_loop(0, N_EXPERTS, add_start, pos_ref[...])


def _dest(pstart, idx_t, pos_t):
    n = idx_t.shape[1]
    spec = pl.BlockSpec((TOP_K, DEST_LANES), lambda i, ps: (0, i))
    return pl.pallas_call(
        _dest_kernel,
        grid_spec=pltpu.PrefetchScalarGridSpec(num_scalar_prefetch=1, grid=(pl.cdiv(n, DEST_LANES),),
                                               in_specs=[spec, spec], out_specs=spec),
        out_shape=jax.ShapeDtypeStruct((TOP_K, n), I32),
        compiler_params=_params(("arbitrary",)),
        name="dest",
    )(pstart, idx_t, pos_t)


def _sc_worker_id():
    return lax.axis_index("s") * 2 + lax.axis_index("c")


def _dispatch_rows(h_first, h_rest, dest_blocks, rows_out):
    nsteps = dest_blocks.shape[0]
    first_steps = h_first.shape[1] // SC_WINDOW
    assert first_steps + h_rest.shape[1] // SC_WINDOW == nsteps
    mesh = plsc.VectorSubcoreMesh(core_axis_name="c", subcore_axis_name="s")

    @functools.partial(
        pl.kernel, mesh=mesh,
        out_type=jax.ShapeDtypeStruct((ROW_CHUNKS, rows_out, CHUNK), F32),
        scratch_types=[pltpu.VMEM((TOP_K, SC_WINDOW), I32),
                       pltpu.VMEM((SC_WINDOW, CHUNK), F32),
                       pltpu.SemaphoreType.DMA],
    )
    def k(xa_hbm, xb_hbm, d_hbm, o_hbm, idx_v, rows_v, sem):
        def move(src_hbm, s, step_in_src):
            base = pl.multiple_of(step_in_src * SC_WINDOW, SC_WINDOW)
            pltpu.sync_copy(d_hbm.at[s], idx_v)
            for c in range(ROW_CHUNKS):
                pltpu.sync_copy(src_hbm.at[c, pl.ds(base, SC_WINDOW)], rows_v)
                for kk in range(TOP_K):
                    pltpu.async_copy(rows_v, o_hbm.at[c].at[idx_v.at[kk]], sem).wait()

        @pl.loop(_sc_worker_id(), nsteps, step=SC_WORKERS)
        def _(s):
            @pl.when(s < first_steps)
            def _():
                move(xa_hbm, s, s)

            @pl.when(s >= first_steps)
            def _():
                move(xb_hbm, s, s - first_steps)

    return k(h_first, h_rest, dest_blocks)


def _gather_rows(y_sorted, dest_blocks):
    nsteps = dest_blocks.shape[0]
    ntok = nsteps * SC_WINDOW
    mesh = plsc.VectorSubcoreMesh(core_axis_name="c", subcore_axis_name="s")

    @functools.partial(
        pl.kernel, mesh=mesh,
        out_type=jax.ShapeDtypeStruct((TOP_K, ROW_CHUNKS, ntok, CHUNK), F32),
        scratch_types=[pltpu.VMEM((TOP_K, SC_WINDOW), I32),
                       pltpu.VMEM((SC_WINDOW, CHUNK), F32),
                       pltpu.SemaphoreType.DMA],
    )
    def k(y_hbm, d_hbm, o_hbm, idx_v, rows_v, sem):
        @pl.loop(_sc_worker_id(), nsteps, step=SC_WORKERS)
        def _(s):
            base = pl.multiple_of(s * SC_WINDOW, SC_WINDOW)
            pltpu.sync_copy(d_hbm.at[s], idx_v)
            for c in range(ROW_CHUNKS):
                for kk in range(TOP_K):
                    pltpu.async_copy(y_hbm.at[c].at[idx_v.at[kk]], rows_v, sem).wait()
                    pltpu.sync_copy(rows_v, o_hbm.at[kk, c, pl.ds(base, SC_WINDOW)])

    return k(y_sorted, dest_blocks)


PAST_LEN = 8192


def _rope_tables(pos):
    half = HEAD_DIM // 2
    inv_freq = ROPE_THETA ** (-jnp.arange(half, dtype=F32) / half)
    ang = pos.astype(F32)[:, None] * inv_freq[None, :]
    cos, sin = jnp.cos(ang), jnp.sin(ang)
    reps = LANES // HEAD_DIM
    return jnp.tile(cos, (1, 2 * reps)), jnp.tile(jnp.concatenate([-sin, sin], axis=1), (1, reps))


def _rope_parts(seq, tm):
    half = HEAD_DIM // 2
    inv_freq = ROPE_THETA ** (-jnp.arange(half, dtype=F32) / half)
    inv_l = jnp.tile(inv_freq, LANES // half)
    start = (jnp.arange(seq // tm, dtype=I32) * tm).astype(F32)
    ang_b = start[:, None] * inv_l[None, :]
    parts = [jnp.stack([jnp.cos(ang_b), jnp.sin(ang_b)], axis=1)]
    r = jnp.arange(tm, dtype=I32)
    for d in DILATIONS:
        n = tm // d
        offset = ((r % n) * d + r // n).astype(F32)
        ang = offset[:, None] * inv_l[None, :]
        parts += [jnp.cos(ang), jnp.sin(ang)]
    return parts


def _kv_tail(kg, vg, keep):
    b, d, l, _ = kg.shape
    n = keep // d

    def natural(t):
        t = t[:, :, l - n:, :].transpose(0, 2, 1, 3)
        return t.reshape(b, keep, HEADS, HEAD_DIM)

    return jnp.stack([natural(kg), natural(vg)], axis=2).astype(F32)[None]


def kernel(x_prompt, x_sample, cache_kv_w128, cache_kv_w512, cache_kv_w2048, state_conv, c_prompt, c_sample,
           g_mix, w_ada_mix, b_ada_mix, w_in, w_dw, b_dw, ln_conv_g, ln_conv_b, w_conv_out, w_att_out, w_o,
           g_ffn, w_ada_ffn, b_ada_ffn, w_router, b_router, w_exp_gate, w_exp_up, w_exp_down,
           w_sh_gate, w_sh_up, w_sh_down, g_final):
    B, S, _ = x_prompt.shape
    ns, T, _ = x_sample.shape
    assert g_mix.shape[0] == 1 and T == 1
    tm = min(TOKEN_TILE, S)
    span = DILATIONS[2] * 16
    assert S % tm == 0 and tm % span == 0 and S % (DILATIONS[2] * Q_BLOCK) == 0
    assert ns % SC_WINDOW == 0 and (B * S) % ns == 0
    ntok = B * S + ns
    caches = (cache_kv_w128, cache_kv_w512, cache_kv_w2048)

    row = lambda v: v.reshape(1, -1)
    w_in_b = w_in[0].astype(BF16)
    w_co_b = w_conv_out[0].astype(BF16)
    w_ao_b = w_att_out[0].astype(BF16)
    w_o_b = w_o[0].astype(BF16)
    wsg_b = w_sh_gate[0].astype(BF16)
    wsu_b = w_sh_up[0].astype(BF16)
    wsd_b = w_sh_down[0].astype(BF16)

    n_c = B + ns
    c_all = jnp.concatenate([c_prompt, c_sample], axis=0)
    c_all = jnp.pad(c_all, ((0, -n_c % 8), (0, 0)))
    mod_mix = _ada(c_all, w_ada_mix[0], b_ada_mix[0])
    mod_ffn = _ada(c_all, w_ada_ffn[0], b_ada_ffn[0])
    modm_p = mod_mix[:B].reshape(B, 1, 3 * D_MODEL)
    modf_p = mod_ffn[:B].reshape(B, 1, 3 * D_MODEL)
    modm_s = mod_mix[B:n_c].reshape(1, ns, 3 * D_MODEL)
    modf_s = mod_ffn[B:n_c].reshape(1, ns, 3 * D_MODEL)

    tables = _rope_parts(S, tm)
    (q0, k0, v0, q1, k1, v1, q2, k2, v2, gc_p, ga_p, utail) = _inproj_prompt(
        x_prompt, modm_p, row(g_mix[0]), w_in_b, tables, w_dw[0], row(b_dw[0]),
        row(ln_conv_g[0]), row(ln_conv_b[0]), w_co_b, tm)
    conv_prompt = utail[:, CONV_HALO - (CONV_WIDTH - 1):][None]

    attn_in = []
    lse_in = []
    kv_prompt = []
    for (qg, kg, vg), d in zip(((q0, k0, v0), (q1, k1, v1), (q2, k2, v2)), DILATIONS):
        l = S // d
        flat = lambda t: t.reshape(B * d, l, D_SLOT)
        o, lse = _attn_prompt(flat(qg), flat(kg), flat(vg))
        if d == 1:
            attn_in.append(o.reshape(B, S, D_SLOT))
            lse_in.append(lse.reshape(B, S, LANES))
        else:
            attn_in.append(o.reshape(B, d, l, D_SLOT))
            lse_in.append(lse.reshape(B, d, l, LANES))
        keep = min(Q_BLOCK * d, S)
        kv_prompt.append(_kv_tail(kg.reshape(B, d, l, D_SLOT), vg.reshape(B, d, l, D_SLOT), keep))

    wr_t = w_router[0].T
    wr_hi = wr_t.astype(BF16)
    wr_lo = (wr_t - wr_hi.astype(F32)).astype(BF16)
    b_col = b_router[0].reshape(N_EXPERTS, 1)
    zero_carry = jnp.zeros((N_EXPERTS, 1), F32)
    h2_p, x2_p, idx_p, wts_p, pos_p, cnt_p = _post(
        x_prompt, modm_p, (*attn_in, *lse_in), gc_p, ga_p, w_ao_b, w_o_b, row(g_ffn[0]), modf_p,
        wr_hi, wr_lo, b_col, wsg_b, wsu_b, wsd_b, zero_carry, tm, True)

    cos_s, sin_s = _rope_tables(jnp.full((1,), PAST_LEN, I32))
    u_s, q_s, k_s, v_s, sgc_s, ga_s = _inproj_sample(
        x_sample.reshape(ns, D_MODEL), mod_mix[B:n_c], row(g_mix[0]), w_in_b, cos_s, sin_s)
    heads = lambda t: t.reshape(ns, N_GROUPS, 1, HEADS, HEAD_DIM)
    qkv_rows = jnp.concatenate([heads(q_s), heads(k_s), heads(v_s)], axis=2)
    kv_sample, outs_s, lses_s = [], [], []
    for gi, (cache, d) in enumerate(zip(caches, DILATIONS)):
        cache_t = jnp.transpose(cache[0], (0, 2, 3, 4, 1))
        new_t, o, lse = _sample_cache(cache_t, qkv_rows, gi, d)
        kv_sample.append(jnp.transpose(new_t, (0, 4, 1, 2, 3))[None])
        outs_s.append(o.reshape(ns, D_SLOT))
        lses_s.append(lse.reshape(ns, D_SLOT))
    state_t = jnp.transpose(state_conv[0], (1, 0, 2))
    gc_s, att_s, cst_t = _sample_mix(state_t, u_s, sgc_s, w_dw[0], row(b_dw[0]), row(ln_conv_g[0]),
                                     row(ln_conv_b[0]), w_co_b, outs_s, lses_s)
    conv_sample = jnp.transpose(cst_t, (1, 0, 2))[None]
    as3 = lambda t: t.reshape(1, ns, t.shape[-1])
    h2_s, x2_s, idx_s, wts_s, pos_s, cnt = _post(
        as3(x_sample.reshape(ns, D_MODEL)), modm_s, (as3(att_s),), as3(gc_s), as3(ga_s), w_ao_b, w_o_b,
        row(g_ffn[0]), modf_s, wr_hi, wr_lo, b_col, wsg_b, wsu_b, wsd_b, cnt_p, ns, False)

    bm = EXPERT_BLOCK
    nblk = (ntok * TOP_K + N_EXPERTS * (bm - 1)) // bm
    counts = cnt[:, 0].astype(I32)
    padded = (counts + bm - 1) // bm * bm
    pend = jnp.cumsum(padded)
    pstart = pend - padded
    idx_all = jnp.concatenate([idx_p, idx_s], axis=1)
    pos_all = jnp.concatenate([pos_p, pos_s], axis=1)
    dest = _dest(pstart, idx_all, pos_all)
    dest_blocks = dest.reshape(TOP_K, ntok // SC_WINDOW, SC_WINDOW).transpose(1, 0, 2)
    wts_tok = jnp.concatenate([wts_p, wts_s], axis=1).T
    blk_row0 = jnp.arange(nblk, dtype=I32) * bm
    blk_e = jnp.minimum(jnp.sum((pend[None, :] <= blk_row0[:, None]).astype(I32), axis=1), N_EXPERTS - 1)
    mine = blk_e[:, None] == jnp.arange(N_EXPERTS, dtype=I32)[None, :]
    cnt_b = jnp.sum(jnp.where(mine, counts[None, :], 0), axis=1)
    start_b = jnp.sum(jnp.where(mine, pstart[None, :], 0), axis=1)
    blk_valid = jnp.clip(cnt_b - (blk_row0 - start_b), 0, bm).astype(I32)
    n_used = (pend[N_EXPERTS - 1] // bm).reshape(1)

    x_sorted = _dispatch_rows(h2_p, h2_s, dest_blocks, nblk * bm)
    y_sorted = _experts(blk_e, blk_valid, n_used, x_sorted, w_exp_gate[0], w_exp_up[0], w_exp_down[0], bm)
    yg = _gather_rows(y_sorted, dest_blocks)

    y_prompt = _final(x2_p, modf_p, wts_tok, yg, row(g_final), 0, tm)
    y_sample = _final(x2_s, modf_s, wts_tok, yg, row(g_final), (B * S) // ns, ns).reshape(ns, 1, D_MODEL)

    return (y_prompt, y_sample, kv_prompt[0], kv_prompt[1], kv_prompt[2], conv_prompt,
            kv_sample[0], kv_sample[1], kv_sample[2], conv_sample)
```

```python
import functools

import jax
import jax.numpy as jnp
from jax import lax
from jax.experimental import pallas as pl
from jax.experimental.pallas import tpu as pltpu
from jax.experimental.pallas import tpu_sc as plsc

F32 = jnp.float32
BF16 = jnp.bfloat16
I32 = jnp.int32

D_MODEL = 1024
D_CONV = 1024
CONV_WIDTH = 31
HEAD_DIM = 64
HEADS = 8
D_SLOT = HEADS * HEAD_DIM
DILATIONS = (1, 4, 16)
N_GROUPS = 3
D_ATT = N_GROUPS * D_SLOT
Q_BLOCK = 128
ROPE_THETA = 10000.0
N_EXPERTS = 256
TOP_K = 8
N_EXPERT_GROUPS = 8
TOPK_GROUPS = 4
GROUP_SIZE = N_EXPERTS // N_EXPERT_GROUPS
D_EXPERT = 256
ROUTED_SCALE = 2.5
RMS_EPS = 1e-6
LN_EPS = 1e-5

COL_A = 0
COL_B = D_CONV
COL_Q = 2 * D_CONV
COL_K = COL_Q + D_ATT
COL_V = COL_K + D_ATT
COL_GC = COL_V + D_ATT
COL_GA = COL_GC + D_MODEL
D_IN = COL_GA + D_MODEL

LANES = 128
CONV_HALO = 32
TOKEN_TILE = 512
EXPERT_BLOCK = 1152
EXPERT_ROW_STEPS = (128, 1024, 1152)
ROW_CHUNKS = 2
CHUNK = D_MODEL // 2 // ROW_CHUNKS
SC_WORKERS = 32
SC_WINDOW = 128
NEG_BIG = -1e30
VMEM_LIMIT = 56 * 1024 * 1024


def _sigmoid(x):
    return 1.0 / (1.0 + jnp.exp(-x))


def _const_spec(shape):
    nd = len(shape)
    return pl.BlockSpec(shape, lambda *_: (0,) * nd, pipeline_mode=pl.Buffered(1))


def _params(sem):
    return pltpu.CompilerParams(dimension_semantics=sem, vmem_limit_bytes=VMEM_LIMIT)


def _ada_kernel(c_ref, w_ref, b_ref, o_ref):
    c = c_ref[...]
    s = (c * _sigmoid(c)).astype(BF16)
    o_ref[...] = jnp.dot(s, w_ref[...].astype(BF16), preferred_element_type=F32) + b_ref[...]


def _ada(c_all, w, b):
    rows = c_all.shape[0]
    cols = w.shape[1]
    tn = 768
    return pl.pallas_call(
        _ada_kernel,
        grid=(cols // tn,),
        in_specs=[pl.BlockSpec((rows, D_MODEL), lambda j: (0, 0)),
                  pl.BlockSpec((D_MODEL, tn), lambda j: (0, j)),
                  pl.BlockSpec((1, tn), lambda j: (0, j))],
        out_specs=pl.BlockSpec((rows, tn), lambda j: (0, j)),
        out_shape=jax.ShapeDtypeStruct((rows, cols), F32),
        compiler_params=_params(("arbitrary",)),
        name="ada",
    )(c_all, w, b.reshape(1, cols))


def _modulated_norm(x, g, mod):
    shift = mod[:, 0:D_MODEL]
    scale = mod[:, D_MODEL:2 * D_MODEL]
    ms = jnp.mean(x * x, axis=-1, keepdims=True)
    return (x * lax.rsqrt(ms + RMS_EPS)) * g * (1.0 + scale) + shift


def _rope(t, cos, sin_signed):
    lane = lax.broadcasted_iota(I32, (t.shape[0], LANES), 1)
    first_half = (lane & (HEAD_DIM - 1)) < (HEAD_DIM // 2)
    outs = []
    for j in range(D_SLOT // LANES):
        ch = t[:, j * LANES:(j + 1) * LANES]
        rot = jnp.where(first_half, pltpu.roll(ch, LANES - HEAD_DIM // 2, 1), pltpu.roll(ch, HEAD_DIM // 2, 1))
        outs.append(ch * cos + rot * sin_signed)
    return jnp.concatenate(outs, axis=1)


def _layer_norm_swish(y, g, b):
    mu = jnp.mean(y, axis=-1, keepdims=True)
    yc = y - mu
    var = jnp.mean(yc * yc, axis=-1, keepdims=True)
    z = yc * lax.rsqrt(var + LN_EPS) * g + b
    return z * _sigmoid(z)


def _inproj_kernel(x_ref, mod_ref, g_ref, w_ref, base_ref, c0_ref, s0_ref, c1_ref, s1_ref, c2_ref, s2_ref,
                   wdw_ref, bdw_ref, lng_ref, lnb_ref, wco_ref,
                   q0_ref, k0_ref, v0_ref, q1_ref, k1_ref, v1_ref, q2_ref, k2_ref, v2_ref,
                   gc_ref, ga_ref, ut_ref,
                   hf_ref, hb0_ref, hb1_ref, hb2_ref, ubuf_ref, *, tm):
    i = pl.program_id(1)
    cw = 512

    h = _modulated_norm(x_ref[...], g_ref[...], mod_ref[...])
    nlc = D_MODEL // LANES
    hb0_ref[...] = h.astype(BF16)

    def mm(hb, col, width=cw):
        return jnp.dot(hb[...], w_ref[:, col:col + width], preferred_element_type=F32)

    @pl.when(i == 0)
    def _():
        ubuf_ref[:, 0:CONV_HALO, :] = jnp.zeros((nlc, CONV_HALO, LANES), F32)

    for c in range(0, D_CONV, cw):
        a = mm(hb0_ref, COL_A + c)
        b = mm(hb0_ref, COL_B + c)
        u = a * _sigmoid(b)
        for cc in range(cw // LANES):
            ubuf_ref[c // LANES + cc, CONV_HALO:CONV_HALO + tm, :] = u[:, cc * LANES:(cc + 1) * LANES]

    for c in range(nlc):
        hf_ref[c] = h[:, c * LANES:(c + 1) * LANES]
    for d, hb in ((DILATIONS[1], hb1_ref), (DILATIONS[2], hb2_ref)):
        n = tm // d
        for r in range(d):
            for c in range(nlc):
                hb[r * n:(r + 1) * n, c * LANES:(c + 1) * LANES] = hf_ref[c, pl.ds(r, n, stride=d), :].astype(BF16)

    groups = ((hb0_ref, 1, c0_ref, s0_ref, q0_ref, k0_ref, v0_ref),
              (hb1_ref, DILATIONS[1], c1_ref, s1_ref, q1_ref, k1_ref, v1_ref),
              (hb2_ref, DILATIONS[2], c2_ref, s2_ref, q2_ref, k2_ref, v2_ref))
    cos_b = base_ref[0:1, :]
    sin_b = base_ref[1:2, :]
    lane = lax.broadcasted_iota(I32, (1, LANES), 1)
    sign = jnp.where((lane & (HEAD_DIM - 1)) < (HEAD_DIM // 2), -1.0, 1.0)
    for gi, (hb, d, c_ref, s_ref, q_ref, k_ref, v_ref) in enumerate(groups):
        cos_w = c_ref[...]
        sin_w = s_ref[...]
        cos = cos_b * cos_w - sin_b * sin_w
        sin = (sin_b * cos_w + cos_b * sin_w) * sign
        n = tm // d

        def put(ref, val):
            vb = val.astype(BF16)
            if d == 1:
                ref[...] = vb
            else:
                for r in range(d):
                    ref[r] = vb[r * n:(r + 1) * n, :]

        put(q_ref, _rope(mm(hb, COL_Q + gi * D_SLOT), cos, sin) * (HEAD_DIM ** -0.5))
        put(k_ref, _rope(mm(hb, COL_K + gi * D_SLOT), cos, sin))
        put(v_ref, mm(hb, COL_V + gi * D_SLOT))

    rb = 32
    first_tap = CONV_HALO - (CONV_WIDTH - 1)

    def conv_rows(r, carry):
        r0 = pl.multiple_of(r * rb, rb)
        for c in range(nlc):
            cs = slice(c * LANES, (c + 1) * LANES)
            acc = jnp.broadcast_to(bdw_ref[:, cs], (rb, LANES))
            for k in range(CONV_WIDTH):
                acc = acc + wdw_ref[k:k + 1, cs] * ubuf_ref[c, pl.ds(r0 + (first_tap + k), rb), :]
            hf_ref[c, pl.ds(r0, rb), :] = acc
        return carry

    lax.fori_loop(0, tm // rb, conv_rows, 0)

    for c in range(0, D_MODEL, cw):
        ga_ref[:, c:c + cw] = _sigmoid(mm(hb0_ref, COL_GA + c)).astype(BF16)
    y_dw = jnp.concatenate([hf_ref[c] for c in range(nlc)], axis=1)
    z = _layer_norm_swish(y_dw, lng_ref[...], lnb_ref[...]).astype(BF16)
    for c in range(0, D_MODEL, cw):
        co = jnp.dot(z, wco_ref[:, c:c + cw], preferred_element_type=F32)
        gc_ref[:, c:c + cw] = (_sigmoid(mm(hb0_ref, COL_GC + c)) * co).astype(BF16)

    for c in range(nlc):
        tail = ubuf_ref[c, tm:tm + CONV_HALO, :]
        ut_ref[:, c * LANES:(c + 1) * LANES] = tail
        ubuf_ref[c, 0:CONV_HALO, :] = tail


def _inproj_prompt(x, mod, g, w_in_b, tables, w_dw, b_dw, ln_g, ln_b, w_co_b, tm):
    B, S, _ = x.shape
    nt = S // tm
    d1, d2 = DILATIONS[1], DILATIONS[2]
    tok = lambda b, i: (b, i, 0)
    tab = _const_spec((tm, LANES))
    in_specs = [
        pl.BlockSpec((None, tm, D_MODEL), tok),
        pl.BlockSpec((None, 1, 3 * D_MODEL), lambda b, i: (b, 0, 0)),
        _const_spec((1, D_MODEL)),
        _const_spec((D_MODEL, D_IN)),
        pl.BlockSpec((None, 2, LANES), lambda b, i: (i, 0, 0)),
        tab, tab, tab, tab, tab, tab,
        _const_spec((CONV_WIDTH, D_CONV)),
        _const_spec((1, D_CONV)),
        _const_spec((1, D_CONV)),
        _const_spec((1, D_CONV)),
        _const_spec((D_CONV, D_MODEL)),
    ]
    nat = pl.BlockSpec((None, tm, D_SLOT), tok)
    st1 = pl.BlockSpec((None, d1, tm // d1, D_SLOT), lambda b, i: (b, 0, i, 0))
    st2 = pl.BlockSpec((None, d2, tm // d2, D_SLOT), lambda b, i: (b, 0, i, 0))
    out_specs = [nat, nat, nat, st1, st1, st1, st2, st2, st2,
                 pl.BlockSpec((None, tm, D_MODEL), tok),
                 pl.BlockSpec((None, tm, D_MODEL), tok),
                 pl.BlockSpec((None, CONV_HALO, D_CONV), lambda b, i: (b, 0, 0))]
    s0 = jax.ShapeDtypeStruct((B, S, D_SLOT), BF16)
    s1 = jax.ShapeDtypeStruct((B, d1, S // d1, D_SLOT), BF16)
    s2 = jax.ShapeDtypeStruct((B, d2, S // d2, D_SLOT), BF16)
    out_shape = [s0, s0, s0, s1, s1, s1, s2, s2, s2,
                 jax.ShapeDtypeStruct((B, S, D_MODEL), BF16),
                 jax.ShapeDtypeStruct((B, S, D_MODEL), BF16),
                 jax.ShapeDtypeStruct((B, CONV_HALO, D_CONV), F32)]
    scratch = [pltpu.VMEM((D_MODEL // LANES, tm, LANES), F32),
               pltpu.VMEM((tm, D_MODEL), BF16),
               pltpu.VMEM((tm, D_MODEL), BF16),
               pltpu.VMEM((tm, D_MODEL), BF16),
               pltpu.VMEM((D_CONV // LANES, tm + CONV_HALO, LANES), F32)]
    return pl.pallas_call(
        functools.partial(_inproj_kernel, tm=tm),
        grid=(B, nt),
        in_specs=in_specs, out_specs=out_specs, out_shape=out_shape,
        scratch_shapes=scratch,
        compiler_params=_params(("arbitrary", "arbitrary")),
        name="inproj",
    )(x, mod, g, w_in_b, *tables, w_dw, b_dw, ln_g, ln_b, w_co_b)


def _attn_kernel(q_ref, kp_ref, kc_ref, vp_ref, vc_ref, o_ref, lse_ref, k_all, v_all, *, nq):
    j = pl.program_id(1)
    k_all[0:Q_BLOCK, :] = kp_ref[...]
    k_all[Q_BLOCK:(nq + 1) * Q_BLOCK, :] = kc_ref[...]
    v_all[0:Q_BLOCK, :] = vp_ref[...]
    v_all[Q_BLOCK:(nq + 1) * Q_BLOCK, :] = vc_ref[...]
    qi = lax.broadcasted_iota(I32, (Q_BLOCK, 2 * Q_BLOCK), 0)
    ki = lax.broadcasted_iota(I32, (Q_BLOCK, 2 * Q_BLOCK), 1)
    back = Q_BLOCK + qi - ki
    in_band = jnp.logical_and(back >= 0, back <= Q_BLOCK)
    first_mask = jnp.logical_and(in_band, jnp.logical_or(ki >= Q_BLOCK, j > 0))
    lane = lax.broadcasted_iota(I32, (Q_BLOCK, LANES), 1)
    even = lax.broadcasted_iota(I32, (1, LANES), 1) < HEAD_DIM
    zero = jnp.zeros((), BF16)
    nt = (((1,), (1,)), ((), ()))
    for sub in range(nq):
        mask = first_mask if sub == 0 else in_band
        qrows = slice(sub * Q_BLOCK, (sub + 1) * Q_BLOCK)
        krows = slice(sub * Q_BLOCK, (sub + 2) * Q_BLOCK)
        scores = []
        for h in range(HEADS):
            pair = slice((h // 2) * LANES, (h // 2 + 1) * LANES)
            mine = even if h % 2 == 0 else jnp.logical_not(even)
            qh = jnp.where(mine, q_ref[qrows, pair], zero)
            scores.append(lax.dot_general(qh, k_all[krows, pair], nt, preferred_element_type=F32))
        probs, dens = [], []
        lse_all = jnp.zeros((Q_BLOCK, LANES), F32)
        for h in range(HEADS):
            s = jnp.where(mask, scores[h], NEG_BIG)
            m = jnp.max(s, axis=-1, keepdims=True)
            p = jnp.exp(s - m)
            den = jnp.sum(p, axis=-1, keepdims=True)
            probs.append(p.astype(BF16))
            dens.append(den)
            lse_all = jnp.where(lane == h, m + jnp.log(den), lse_all)
        for hp in range(HEADS // 2):
            pair = slice(hp * LANES, (hp + 1) * LANES)
            v2 = v_all[krows, pair]
            oe = jnp.dot(probs[2 * hp], v2, preferred_element_type=F32) / dens[2 * hp]
            oo = jnp.dot(probs[2 * hp + 1], v2, preferred_element_type=F32) / dens[2 * hp + 1]
            o_ref[qrows, pair] = jnp.where(even, oe, oo).astype(BF16)
        lse_ref[qrows, :] = lse_all


def _attn_prompt(q, k, v):
    ns, L, _ = q.shape
    nq = max(c for c in (4, 2, 1) if (L // Q_BLOCK) % c == 0)
    rows = nq * Q_BLOCK
    cur = pl.BlockSpec((None, rows, D_SLOT), lambda n, j: (n, j, 0))
    prev = pl.BlockSpec((None, Q_BLOCK, D_SLOT), lambda n, j: (n, jnp.maximum(j * nq - 1, 0), 0))
    return pl.pallas_call(
        functools.partial(_attn_kernel, nq=nq),
        grid=(ns, L // rows),
        in_specs=[cur, prev, cur, prev, cur],
        out_specs=[cur, pl.BlockSpec((None, rows, LANES), lambda n, j: (n, j, 0))],
        out_shape=[jax.ShapeDtypeStruct((ns, L, D_SLOT), BF16),
                   jax.ShapeDtypeStruct((ns, L, LANES), F32)],
        scratch_shapes=[pltpu.VMEM((rows + Q_BLOCK, D_SLOT), BF16), pltpu.VMEM((rows + Q_BLOCK, D_SLOT), BF16)],
        compiler_params=_params(("arbitrary", "arbitrary")),
        name="attn",
    )(q, k, k, v, v)


def _route_t(logits_t, b_col, carry, tm):
    ninf = -jnp.inf
    scores = _sigmoid(logits_t)
    sel = scores + b_col
    rowf = lax.broadcasted_iota(I32, (N_EXPERTS, tm), 0).astype(F32)
    past_end = float(N_EXPERTS)

    def first_max(x, rows):
        m = jnp.max(x, axis=0, keepdims=True)
        return m, jnp.min(jnp.where(x == m, rows, past_end), axis=0, keepdims=True)

    gs = []
    rowg = lax.broadcasted_iota(I32, (GROUP_SIZE, tm), 0).astype(F32)
    for g in range(N_EXPERT_GROUPS):
        rs = slice(g * GROUP_SIZE, (g + 1) * GROUP_SIZE)
        m1, i1 = first_max(sel[rs], rowg)
        m2 = jnp.max(jnp.where(rowg == i1, ninf, sel[rs]), axis=0, keepdims=True)
        gs.append(m1 + m2)
    pieces = []
    for g in range(N_EXPERT_GROUPS):
        beaten = jnp.zeros((1, tm), F32)
        for g2 in range(N_EXPERT_GROUPS):
            if g2 != g:
                better = gs[g2] >= gs[g] if g2 < g else gs[g2] > gs[g]
                beaten = beaten + jnp.where(better, 1.0, 0.0)
        rs = slice(g * GROUP_SIZE, (g + 1) * GROUP_SIZE)
        pieces.append(jnp.where(beaten < TOPK_GROUPS, sel[rs], ninf))
    selm = jnp.concatenate(pieces, axis=0)

    idx_rows, w_rows = [], []
    picked = jnp.zeros((N_EXPERTS, tm), F32)
    for _ in range(TOP_K):
        _, ik = first_max(selm, rowf)
        hit = rowf == ik
        w_rows.append(jnp.sum(jnp.where(hit, scores, 0.0), axis=0, keepdims=True))
        selm = jnp.where(hit, ninf, selm)
        picked = jnp.where(hit, 1.0, picked)
        idx_rows.append(ik)
    wsum = w_rows[0]
    for wk in w_rows[1:]:
        wsum = wsum + wk
    denom = wsum + 1e-20

    ti = lax.broadcasted_iota(I32, (tm, tm), 0)
    tj = lax.broadcasted_iota(I32, (tm, tm), 1)
    earlier = jnp.where(ti < tj, 1.0, 0.0).astype(BF16)
    before = jnp.dot(picked.astype(BF16), earlier, preferred_element_type=F32) + carry
    new_carry = carry + jnp.sum(picked, axis=1, keepdims=True)
    pos_rows = [jnp.sum(jnp.where(rowf == ik, before, 0.0), axis=0, keepdims=True) for ik in idx_rows]

    idx_o = jnp.concatenate(idx_rows, axis=0).astype(I32)
    w_o = jnp.concatenate([wk / denom * ROUTED_SCALE for wk in w_rows], axis=0)
    pos_o = jnp.concatenate(pos_rows, axis=0).astype(I32)
    return idx_o, w_o, pos_o, new_carry


def _pack_rows(xb):
    bits = lax.bitcast_convert_type(xb.astype(F32), jnp.uint32)
    half = D_MODEL // 2
    word = bits[:, half:] | (bits[:, :half] >> 16)
    return lax.bitcast_convert_type(word, F32)


def _unpack_rows(words):
    bits = lax.bitcast_convert_type(words, jnp.uint32)
    lo = lax.bitcast_convert_type(bits << 16, F32)
    hi = lax.bitcast_convert_type(bits & jnp.uint32(0xFFFF0000), F32)
    return jnp.concatenate([lo, hi], axis=1)


def _post_kernel(*refs, tm, combine):
    refs = list(refs)
    x_ref, modm_ref = refs[0:2]
    p = 2
    if combine:
        o0_ref, o1_ref, o2_ref, l0_ref, l1_ref, l2_ref = refs[p:p + 6]
        p += 6
    else:
        att_ref = refs[p]
        p += 1
    (gc_ref, ga_ref, wao_ref, wo_ref, gffn_ref, modf_ref, wrh_ref, wrl_ref, br_ref,
     wsg_ref, wsu_ref, wsd_ref, cin_ref) = refs[p:p + 13]
    p += 13
    h2_ref, x2_ref, idx_ref, wts_ref, pos_ref, cnt_ref = refs[p:p + 6]
    p += 6
    if combine:
        o1n_ref, o2n_ref, l1n_ref, l2n_ref = refs[p:p + 4]

    first = jnp.logical_and(pl.program_id(0) == 0, pl.program_id(1) == 0)

    @pl.when(first)
    def _():
        cnt_ref[...] = cin_ref[...]

    if combine:
        nsc = D_SLOT // LANES
        for d, o_ref, l_ref, on_ref, ln_ref in ((DILATIONS[1], o1_ref, l1_ref, o1n_ref, l1n_ref),
                                                (DILATIONS[2], o2_ref, l2_ref, o2n_ref, l2n_ref)):
            n = tm // d
            for r in range(d):
                ln_ref[pl.ds(r, n, stride=d), :] = l_ref[r]
                orow = o_ref[r].astype(F32)
                for c in range(nsc):
                    on_ref[c, pl.ds(r, n, stride=d), :] = orow[:, c * LANES:(c + 1) * LANES]
        l0 = l0_ref[...]
        l1 = l1n_ref[...]
        l2 = l2n_ref[...]
        mx = jnp.maximum(jnp.maximum(l0, l1), l2)
        e0 = jnp.exp(l0 - mx)
        e1 = jnp.exp(l1 - mx)
        e2 = jnp.exp(l2 - mx)
        esum = e0 + e1 + e2
        even = lax.broadcasted_iota(I32, (tm, LANES), 1) < HEAD_DIM

        def expand(w):
            wide = lambda h: jnp.broadcast_to(w[:, h:h + 1], (tm, LANES))
            return jnp.concatenate([jnp.where(even, wide(2 * c), wide(2 * c + 1)) for c in range(nsc)], axis=1)

        o1n = jnp.concatenate([o1n_ref[c] for c in range(nsc)], axis=1)
        o2n = jnp.concatenate([o2n_ref[c] for c in range(nsc)], axis=1)
        att = (expand(e0 / esum) * o0_ref[...].astype(F32) + expand(e1 / esum) * o1n
               + expand(e2 / esum) * o2n)
    else:
        att = att_ref[...]

    att_out = jnp.dot(att.astype(BF16), wao_ref[...], preferred_element_type=F32)
    merged = gc_ref[...] + ga_ref[...] * att_out.astype(BF16)
    y = jnp.dot(merged, wo_ref[...], preferred_element_type=F32)
    x1 = x_ref[...] + modm_ref[:, 2 * D_MODEL:3 * D_MODEL] * y

    modf = modf_ref[...]
    h2 = _modulated_norm(x1, gffn_ref[...], modf)
    h2b = h2.astype(BF16)
    words = _pack_rows(h2b)
    for c in range(ROW_CHUNKS):
        h2_ref[c] = words[:, c * CHUNK:(c + 1) * CHUNK]
    sg = jnp.dot(h2b, wsg_ref[...], preferred_element_type=F32)
    su = jnp.dot(h2b, wsu_ref[...], preferred_element_type=F32)
    sh = jnp.dot((sg * _sigmoid(sg) * su).astype(BF16), wsd_ref[...], preferred_element_type=F32)
    x2_ref[...] = x1 + modf[:, 2 * D_MODEL:3 * D_MODEL] * sh

    h2l = (h2 - h2b.astype(F32)).astype(BF16)
    nt = (((1,), (1,)), ((), ()))
    logits_t = (lax.dot_general(wrh_ref[...], h2b, nt, preferred_element_type=F32)
                + lax.dot_general(wrl_ref[...], h2b, nt, preferred_element_type=F32)
                + lax.dot_general(wrh_ref[...], h2l, nt, preferred_element_type=F32))
    carry = cnt_ref[...]
    for lc in range(tm // LANES):
        ls = slice(lc * LANES, (lc + 1) * LANES)
        idx_o, w_o, pos_o, carry = _route_t(logits_t[:, ls], br_ref[...], carry, LANES)
        idx_ref[:, ls] = idx_o
        wts_ref[:, ls] = w_o
        pos_ref[:, ls] = pos_o
    cnt_ref[...] = carry


def _post(x, modm, attn_inputs, gc, ga, w_ao_b, w_o_b, g_ffn, modf, wr_hi, wr_lo, b_router,
          wsg_b, wsu_b, wsd_b, carry_in, tm, combine):
    B, S, _ = x.shape
    nt = S // tm
    tok = lambda b, i: (b, i, 0)
    stream = lambda b, i: (b, 0, i, 0)
    mod_rows = modm.shape[1]
    mod_spec = (pl.BlockSpec((None, 1, 3 * D_MODEL), lambda b, i: (b, 0, 0)) if mod_rows == 1
                else pl.BlockSpec((None, tm, 3 * D_MODEL), tok))
    full = pl.BlockSpec((None, tm, D_MODEL), tok)
    in_specs = [full, mod_spec]
    scratch = []
    if combine:
        d1, d2 = DILATIONS[1], DILATIONS[2]
        in_specs += [pl.BlockSpec((None, tm, D_SLOT), tok),
                     pl.BlockSpec((None, d1, tm // d1, D_SLOT), stream),
                     pl.BlockSpec((None, d2, tm // d2, D_SLOT), stream),
                     pl.BlockSpec((None, tm, LANES), tok),
                     pl.BlockSpec((None, d1, tm // d1, LANES), stream),
                     pl.BlockSpec((None, d2, tm // d2, LANES), stream)]
        scratch += [pltpu.VMEM((D_SLOT // LANES, tm, LANES), F32),
                    pltpu.VMEM((D_SLOT // LANES, tm, LANES), F32),
                    pltpu.VMEM((tm, LANES), F32),
                    pltpu.VMEM((tm, LANES), F32)]
    else:
        in_specs += [pl.BlockSpec((None, tm, D_SLOT), tok)]
    in_specs += [full, full,
                 _const_spec((D_SLOT, D_MODEL)), _const_spec((D_MODEL, D_MODEL)),
                 _const_spec((1, D_MODEL)), mod_spec,
                 _const_spec((N_EXPERTS, D_MODEL)), _const_spec((N_EXPERTS, D_MODEL)),
                 _const_spec((N_EXPERTS, 1)),
                 _const_spec((D_MODEL, D_EXPERT)), _const_spec((D_MODEL, D_EXPERT)),
                 _const_spec((D_EXPERT, D_MODEL)), _const_spec((N_EXPERTS, 1))]
    pack = pl.BlockSpec((TOP_K, tm), lambda b, i: (0, b * nt + i))
    chunked = pl.BlockSpec((ROW_CHUNKS, tm, CHUNK), lambda b, i: (0, b * nt + i, 0))
    out_specs = [chunked, full, pack, pack, pack, pl.BlockSpec((N_EXPERTS, 1), lambda b, i: (0, 0))]
    out_shape = [jax.ShapeDtypeStruct((ROW_CHUNKS, B * S, CHUNK), F32),
                 jax.ShapeDtypeStruct((B, S, D_MODEL), F32),
                 jax.ShapeDtypeStruct((TOP_K, B * S), I32),
                 jax.ShapeDtypeStruct((TOP_K, B * S), F32),
                 jax.ShapeDtypeStruct((TOP_K, B * S), I32),
                 jax.ShapeDtypeStruct((N_EXPERTS, 1), F32)]
    return pl.pallas_call(
        functools.partial(_post_kernel, tm=tm, combine=combine),
        grid=(B, nt),
        in_specs=in_specs, out_specs=out_specs, out_shape=out_shape,
        scratch_shapes=scratch,
        compiler_params=_params(("arbitrary", "arbitrary")),
        name="post_prompt" if combine else "post_sample",
    )(x, modm, *attn_inputs, gc, ga, w_ao_b, w_o_b, g_ffn, modf, wr_hi, wr_lo, b_router,
      wsg_b, wsu_b, wsd_b, carry_in)


def _inproj_sample_kernel(x_ref, mod_ref, g_ref, w_ref, cos_ref, sin_ref,
                          u_ref, q_ref, k_ref, v_ref, sgc_ref, ga_ref):
    cw = 512
    hb = _modulated_norm(x_ref[...], g_ref[...], mod_ref[...]).astype(BF16)

    def mm(col, width=cw):
        return jnp.dot(hb, w_ref[:, col:col + width], preferred_element_type=F32)

    cos = cos_ref[...]
    sin = sin_ref[...]
    for c in range(0, D_CONV, cw):
        u_ref[:, c:c + cw] = mm(COL_A + c) * _sigmoid(mm(COL_B + c))
        sgc_ref[:, c:c + cw] = _sigmoid(mm(COL_GC + c))
        ga_ref[:, c:c + cw] = _sigmoid(mm(COL_GA + c)).astype(BF16)
    for gi in range(N_GROUPS):
        cs = slice(gi * D_SLOT, (gi + 1) * D_SLOT)
        q_ref[:, cs] = _rope(mm(COL_Q + gi * D_SLOT), cos, sin) * (HEAD_DIM ** -0.5)
        k_ref[:, cs] = _rope(mm(COL_K + gi * D_SLOT), cos, sin)
        v_ref[:, cs] = mm(COL_V + gi * D_SLOT)


def _inproj_sample(x, mod, g, w_in_b, cos, sin):
    ns = x.shape[0]
    whole = lambda shape: pl.BlockSpec(shape, lambda i: (0,) * len(shape))
    f = lambda cols, dt=F32: jax.ShapeDtypeStruct((ns, cols), dt)
    return pl.pallas_call(
        _inproj_sample_kernel,
        grid=(1,),
        in_specs=[whole((ns, D_MODEL)), whole((ns, 3 * D_MODEL)), _const_spec((1, D_MODEL)),
                  _const_spec((D_MODEL, D_IN)), whole((1, LANES)), whole((1, LANES))],
        out_specs=[whole((ns, D_CONV)), whole((ns, D_ATT)), whole((ns, D_ATT)), whole((ns, D_ATT)),
                   whole((ns, D_MODEL)), whole((ns, D_MODEL))],
        out_shape=[f(D_CONV), f(D_ATT), f(D_ATT), f(D_ATT), f(D_MODEL), f(D_MODEL, BF16)],
        compiler_params=_params(("arbitrary",)),
        name="inproj_sample",
    )(x, mod, g, w_in_b, cos, sin)


def _sample_mix_kernel(state_ref, u_ref, sgc_ref, wdw_ref, bdw_ref, lng_ref, lnb_ref, wco_ref,
                       o0_ref, o1_ref, o2_ref, l0_ref, l1_ref, l2_ref,
                       gc_ref, att_ref, cst_ref):
    nprev = CONV_WIDTH - 1
    u = u_ref[...]
    y = wdw_ref[nprev:CONV_WIDTH, :] * u + bdw_ref[...]
    for k in range(nprev):
        y = y + wdw_ref[k:k + 1, :] * state_ref[k]
    z = _layer_norm_swish(y, lng_ref[...], lnb_ref[...]).astype(BF16)
    co = jnp.dot(z, wco_ref[...], preferred_element_type=F32)
    gc_ref[...] = (sgc_ref[...] * co).astype(BF16)
    for k in range(nprev - 1):
        cst_ref[k] = state_ref[k + 1]
    cst_ref[nprev - 1] = u

    l0 = l0_ref[...]
    l1 = l1_ref[...]
    l2 = l2_ref[...]
    mx = jnp.maximum(jnp.maximum(l0, l1), l2)
    e0 = jnp.exp(l0 - mx)
    e1 = jnp.exp(l1 - mx)
    e2 = jnp.exp(l2 - mx)
    esum = e0 + e1 + e2
    att_ref[...] = (e0 / esum) * o0_ref[...] + (e1 / esum) * o1_ref[...] + (e2 / esum) * o2_ref[...]


def _sample_mix(state, u, sgc, w_dw, b_dw, ln_g, ln_b, w_co_b, outs, lses):
    ns = state.shape[1]
    nprev = CONV_WIDTH - 1
    whole = lambda *shape: pl.BlockSpec(shape, lambda i: (0,) * len(shape))
    slot = whole(ns, D_SLOT)
    return pl.pallas_call(
        _sample_mix_kernel,
        grid=(1,),
        in_specs=[whole(nprev, ns, D_CONV), whole(ns, D_CONV), whole(ns, D_MODEL),
                  whole(CONV_WIDTH, D_CONV), whole(1, D_CONV), whole(1, D_CONV), whole(1, D_CONV),
                  whole(D_CONV, D_MODEL), slot, slot, slot, slot, slot, slot],
        out_specs=[whole(ns, D_MODEL), slot, whole(nprev, ns, D_CONV)],
        out_shape=[jax.ShapeDtypeStruct((ns, D_MODEL), BF16),
                   jax.ShapeDtypeStruct((ns, D_SLOT), F32),
                   jax.ShapeDtypeStruct((nprev, ns, D_CONV), F32)],
        compiler_params=_params(("arbitrary",)),
        name="sample_mix",
    )(state, u, sgc, w_dw, b_dw, ln_g, ln_b, w_co_b, *outs, *lses)


def _split3(x):
    p1 = x.astype(BF16)
    r1 = x - p1.astype(F32)
    p2 = r1.astype(BF16)
    p3 = (r1 - p2.astype(F32)).astype(BF16)
    return p1, p2, p3


def _sample_cache_kernel(c_ref, r_ref, co_ref, o_ref, l_ref, *, n, dil, sb):
    lane = lax.broadcasted_iota(I32, (1, n), 1)
    back = n - lane
    use = jnp.logical_and(jnp.logical_and(back % dil == 0, back <= Q_BLOCK * dil), back >= dil)
    last = lax.broadcasted_iota(I32, (HEAD_DIM, n), 1) == n - 1
    nt = (((1,), (1,)), ((), ()))
    eye_d = (lax.broadcasted_iota(I32, (HEAD_DIM, HEAD_DIM), 0)
             == lax.broadcasted_iota(I32, (HEAD_DIM, HEAD_DIM), 1)).astype(BF16)
    kv_rows = r_ref[:, 1:3].reshape(sb * 2 * HEADS, HEAD_DIM)
    kv_cols = sum(lax.dot_general(eye_d, part, nt, preferred_element_type=F32) for part in _split3(kv_rows))

    for b in range(sb):
        q = r_ref[b, 0]
        qb = q.astype(BF16)
        rows = []
        for h in range(HEADS):
            kh = c_ref[b, 0, h].astype(BF16)
            rows.append(jnp.dot(qb, kh, preferred_element_type=F32)[h:h + 1, :])
        s = jnp.where(use, jnp.concatenate(rows, axis=0), NEG_BIG)
        sn = jnp.sum(q * r_ref[b, 1], axis=1, keepdims=True)
        m = jnp.maximum(jnp.max(s, axis=1, keepdims=True), sn)
        p = jnp.exp(s - m)
        pn = jnp.exp(sn - m)
        den = jnp.sum(p, axis=1, keepdims=True) + pn
        pb = p.astype(BF16)
        outs = []
        for h in range(HEADS):
            vh = c_ref[b, 1, h].astype(BF16)
            outs.append(lax.dot_general(pb, vh, nt, preferred_element_type=F32)[h:h + 1, :])
        o_ref[b] = (jnp.concatenate(outs, axis=0) + pn * r_ref[b, 2]) / den
        l_ref[b] = jnp.broadcast_to(m + jnp.log(den), (HEADS, HEAD_DIM))
        for h in range(HEADS):
            for kind in range(2):
                j = (b * 2 + kind) * HEADS + h
                co_ref[b, kind, h] = jnp.where(last, kv_cols[:, j:j + 1], pltpu.roll(c_ref[b, kind, h], n - 1, 1))


def _sample_cache(cache_t, qkv_rows, gi, dil):
    ns, _, _, _, n = cache_t.shape
    seq_bytes = 2 * HEADS * HEAD_DIM * n * 4
    sb = max(1, min(4, (4 * 1024 * 1024) // seq_bytes))
    assert ns % sb == 0
    cspec = pl.BlockSpec((sb, 2, HEADS, HEAD_DIM, n), lambda b: (b, 0, 0, 0, 0))
    rspec = pl.BlockSpec((sb, None, 3, HEADS, HEAD_DIM), lambda b: (b, gi, 0, 0, 0))
    ospec = pl.BlockSpec((sb, HEADS, HEAD_DIM), lambda b: (b, 0, 0))
    slot = jax.ShapeDtypeStruct((ns, HEADS, HEAD_DIM), F32)
    return pl.pallas_call(
        functools.partial(_sample_cache_kernel, n=n, dil=dil, sb=sb),
        grid=(ns // sb,),
        in_specs=[cspec, rspec],
        out_specs=[cspec, ospec, ospec],
        out_shape=[jax.ShapeDtypeStruct(cache_t.shape, F32), slot, slot],
        compiler_params=_params(("arbitrary",)),
        name="sample_cache",
    )(cache_t, qkv_rows)


def _experts_kernel(be_ref, bv_ref, nu_ref, x_ref, wg_ref, wu_ref, wd_ref, y_ref, wgb_ref, wub_ref, wdb_ref, *, bm):
    i = pl.program_id(0)
    e = be_ref[i]
    prev = be_ref[jnp.maximum(i - 1, 0)]
    valid = jnp.where(i < nu_ref[0], bv_ref[i], 0)

    @pl.when(jnp.logical_and(valid > 0, jnp.logical_or(i == 0, e != prev)))
    def _():
        wgb_ref[...] = wg_ref[...].astype(BF16)
        wub_ref[...] = wu_ref[...].astype(BF16)
        wdb_ref[...] = wd_ref[...].astype(BF16)

    def run(rows):
        words = jnp.concatenate([x_ref[c, 0:rows, :] for c in range(ROW_CHUNKS)], axis=1)
        row = lax.broadcasted_iota(I32, (rows, D_MODEL), 0)
        x = jnp.where(row < valid, _unpack_rows(words), 0.0).astype(BF16)
        g = jnp.dot(x, wgb_ref[...], preferred_element_type=F32)
        u = jnp.dot(x, wub_ref[...], preferred_element_type=F32)
        a = (g * _sigmoid(g) * u).astype(BF16)
        y = jnp.dot(a, wdb_ref[...], preferred_element_type=F32)
        yw = _pack_rows(y.astype(BF16))
        for c in range(ROW_CHUNKS):
            y_ref[c, 0:rows, :] = yw[:, c * CHUNK:(c + 1) * CHUNK]
            if rows < bm:
                y_ref[c, rows:bm, :] = jnp.zeros((bm - rows, CHUNK), F32)

    lower = 0
    for rows in EXPERT_ROW_STEPS:
        upper_ok = valid <= rows if rows < bm else True

        @pl.when(jnp.logical_and(valid > lower, upper_ok))
        def _(rows=rows):
            run(rows)

        lower = rows


def _experts(blk_e, blk_valid, n_used, x_sorted, w_g, w_u, w_d, bm):
    rows = x_sorted.shape[1]
    nblk = rows // bm
    blk = lambda i, be, bv, nu: (0, jnp.minimum(i, nu[0] - 1), 0)
    wsel = lambda i, be, bv, nu: (be[jnp.minimum(i, nu[0] - 1)], 0, 0)
    grid_spec = pltpu.PrefetchScalarGridSpec(
        num_scalar_prefetch=3,
        grid=(nblk,),
        in_specs=[pl.BlockSpec((ROW_CHUNKS, bm, CHUNK), blk),
                  pl.BlockSpec((None, D_MODEL, D_EXPERT), wsel),
                  pl.BlockSpec((None, D_MODEL, D_EXPERT), wsel),
                  pl.BlockSpec((None, D_EXPERT, D_MODEL), wsel)],
        out_specs=pl.BlockSpec((ROW_CHUNKS, bm, CHUNK), blk),
        scratch_shapes=[pltpu.VMEM((D_MODEL, D_EXPERT), BF16), pltpu.VMEM((D_MODEL, D_EXPERT), BF16),
                        pltpu.VMEM((D_EXPERT, D_MODEL), BF16)])
    return pl.pallas_call(
        functools.partial(_experts_kernel, bm=bm),
        grid_spec=grid_spec,
        out_shape=jax.ShapeDtypeStruct((ROW_CHUNKS, rows, CHUNK), F32),
        compiler_params=_params(("arbitrary",)),
        name="experts",
    )(blk_e, blk_valid, n_used, x_sorted, w_g, w_u, w_d)


def _final_kernel(x2_ref, modf_ref, wts_ref, yg_ref, gfin_ref, o_ref):
    w = wts_ref[...]

    def picked(k):
        return _unpack_rows(jnp.concatenate([yg_ref[k, c] for c in range(ROW_CHUNKS)], axis=1))

    routed = w[:, 0:1] * picked(0)
    for k in range(1, TOP_K):
        routed = routed + w[:, k:k + 1] * picked(k)
    x = x2_ref[...] + modf_ref[:, 2 * D_MODEL:3 * D_MODEL] * routed
    ms = jnp.mean(x * x, axis=-1, keepdims=True)
    o_ref[...] = x * lax.rsqrt(ms + RMS_EPS) * gfin_ref[...]


def _final(x2, modf, wts, yg, g_final, tok_block_offset, tm):
    B, S, _ = x2.shape
    nt = S // tm
    tok = lambda b, i: (b, i, 0)
    mod_rows = modf.shape[1]
    mod_spec = (pl.BlockSpec((None, 1, 3 * D_MODEL), lambda b, i: (b, 0, 0)) if mod_rows == 1
                else pl.BlockSpec((None, tm, 3 * D_MODEL), tok))
    return pl.pallas_call(
        _final_kernel,
        grid=(B, nt),
        in_specs=[pl.BlockSpec((None, tm, D_MODEL), tok), mod_spec,
                  pl.BlockSpec((tm, TOP_K), lambda b, i: (tok_block_offset + b * nt + i, 0)),
                  pl.BlockSpec((TOP_K, ROW_CHUNKS, tm, CHUNK),
                               lambda b, i: (0, 0, tok_block_offset + b * nt + i, 0)),
                  _const_spec((1, D_MODEL))],
        out_specs=pl.BlockSpec((None, tm, D_MODEL), tok),
        out_shape=jax.ShapeDtypeStruct((B, S, D_MODEL), F32),
        compiler_params=_params(("arbitrary", "arbitrary")),
        name="final",
    )(x2, modf, wts, yg, g_final)


DEST_LANES = 1024


def _dest_kernel(pstart_ref, idx_ref, pos_ref, o_ref):
    idx = idx_ref[...]

    def add_start(e, acc):
        return acc + jnp.where(idx == e, pstart_ref[e], 0)

    o_ref[...] = lax.fori_loop(0, N_EXPERTS, add_start, pos_ref[...])


def _dest(pstart, idx_t, pos_t):
    n = idx_t.shape[1]
    spec = pl.BlockSpec((TOP_K, DEST_LANES), lambda i, ps: (0, i))
    return pl.pallas_call(
        _dest_kernel,
        grid_spec=pltpu.PrefetchScalarGridSpec(num_scalar_prefetch=1, grid=(pl.cdiv(n, DEST_LANES),),
                                               in_specs=[spec, spec], out_specs=spec),
        out_shape=jax.ShapeDtypeStruct((TOP_K, n), I32),
        compiler_params=_params(("arbitrary",)),
        name="dest",
    )(pstart, idx_t, pos_t)


def _sc_worker_id():
    return lax.axis_index("s") * 2 + lax.axis_index("c")


def _dispatch_rows(h_first, h_rest, dest_blocks, rows_out):
    nsteps = dest_blocks.shape[0]
    first_steps = h_first.shape[1] // SC_WINDOW
    assert first_steps + h_rest.shape[1] // SC_WINDOW == nsteps
    mesh = plsc.VectorSubcoreMesh(core_axis_name="c", subcore_axis_name="s")

    @functools.partial(
        pl.kernel, mesh=mesh,
        out_type=jax.ShapeDtypeStruct((ROW_CHUNKS, rows_out, CHUNK), F32),
        scratch_types=[pltpu.VMEM((TOP_K, SC_WINDOW), I32),
                       pltpu.VMEM((SC_WINDOW, CHUNK), F32),
                       pltpu.SemaphoreType.DMA],
    )
    def k(xa_hbm, xb_hbm, d_hbm, o_hbm, idx_v, rows_v, sem):
        def move(src_hbm, s, step_in_src):
            base = pl.multiple_of(step_in_src * SC_WINDOW, SC_WINDOW)
            pltpu.sync_copy(d_hbm.at[s], idx_v)
            for c in range(ROW_CHUNKS):
                pltpu.sync_copy(src_hbm.at[c, pl.ds(base, SC_WINDOW)], rows_v)
                scatters = [pltpu.async_copy(rows_v, o_hbm.at[c].at[idx_v.at[kk]], sem) for kk in range(TOP_K)]
                for cp in scatters:
                    cp.wait()

        @pl.loop(_sc_worker_id(), nsteps, step=SC_WORKERS)
        def _(s):
            @pl.when(s < first_steps)
            def _():
                move(xa_hbm, s, s)

            @pl.when(s >= first_steps)
            def _():
                move(xb_hbm, s, s - first_steps)

    return k(h_first, h_rest, dest_blocks)


def _gather_rows(y_sorted, dest_blocks):
    nsteps = dest_blocks.shape[0]
    ntok = nsteps * SC_WINDOW
    mesh = plsc.VectorSubcoreMesh(core_axis_name="c", subcore_axis_name="s")

    @functools.partial(
        pl.kernel, mesh=mesh,
        out_type=jax.ShapeDtypeStruct((TOP_K, ROW_CHUNKS, ntok, CHUNK), F32),
        scratch_types=[pltpu.VMEM((TOP_K, SC_WINDOW), I32),
                       pltpu.VMEM((SC_WINDOW, CHUNK), F32),
                       pltpu.VMEM((SC_WINDOW, CHUNK), F32),
                       pltpu.SemaphoreType.DMA((2,)),
                       pltpu.SemaphoreType.DMA((2,))],
    )
    def k(y_hbm, d_hbm, o_hbm, idx_v, buf0, buf1, sem_in, sem_out):
        bufs = (buf0, buf1)
        items = [(c, kk) for c in range(ROW_CHUNKS) for kk in range(TOP_K)]

        @pl.loop(_sc_worker_id(), nsteps, step=SC_WORKERS)
        def _(s):
            base = pl.multiple_of(s * SC_WINDOW, SC_WINDOW)
            pltpu.sync_copy(d_hbm.at[s], idx_v)

            def fetch(i):
                c, kk = items[i]
                return pltpu.async_copy(y_hbm.at[c].at[idx_v.at[kk]], bufs[i % 2], sem_in.at[i % 2])

            def store(i):
                c, kk = items[i]
                return pltpu.async_copy(bufs[i % 2], o_hbm.at[kk, c, pl.ds(base, SC_WINDOW)], sem_out.at[i % 2])

            pending_store = [None, None]
            fetched = fetch(0)
            for i in range(len(items)):
                nxt = None
                if i + 1 < len(items):
                    if pending_store[(i + 1) % 2] is not None:
                        pending_store[(i + 1) % 2].wait()
                    nxt = fetch(i + 1)
                fetched.wait()
                pending_store[i % 2] = store(i)
                fetched = nxt
            pending_store[0].wait()
            pending_store[1].wait()

    return k(y_sorted, dest_blocks)


PAST_LEN = 8192


def _rope_tables(pos):
    half = HEAD_DIM // 2
    inv_freq = ROPE_THETA ** (-jnp.arange(half, dtype=F32) / half)
    ang = pos.astype(F32)[:, None] * inv_freq[None, :]
    cos, sin = jnp.cos(ang), jnp.sin(ang)
    reps = LANES // HEAD_DIM
    return jnp.tile(cos, (1, 2 * reps)), jnp.tile(jnp.concatenate([-sin, sin], axis=1), (1, reps))


def _rope_parts(seq, tm):
    half = HEAD_DIM // 2
    inv_freq = ROPE_THETA ** (-jnp.arange(half, dtype=F32) / half)
    inv_l = jnp.tile(inv_freq, LANES // half)
    start = (jnp.arange(seq // tm, dtype=I32) * tm).astype(F32)
    ang_b = start[:, None] * inv_l[None, :]
    parts = [jnp.stack([jnp.cos(ang_b), jnp.sin(ang_b)], axis=1)]
    r = jnp.arange(tm, dtype=I32)
    for d in DILATIONS:
        n = tm // d
        offset = ((r % n) * d + r // n).astype(F32)
        ang = offset[:, None] * inv_l[None, :]
        parts += [jnp.cos(ang), jnp.sin(ang)]
    return parts


def _kv_tail(kg, vg, keep):
    b, d, l, _ = kg.shape
    n = keep // d

    def natural(t):
        t = t[:, :, l - n:, :].transpose(0, 2, 1, 3)
        return t.reshape(b, keep, HEADS, HEAD_DIM)

    return jnp.stack([natural(kg), natural(vg)], axis=2).astype(F32)[None]


def kernel(x_prompt, x_sample, cache_kv_w128, cache_kv_w512, cache_kv_w2048, state_conv, c_prompt, c_sample,
           g_mix, w_ada_mix, b_ada_mix, w_in, w_dw, b_dw, ln_conv_g, ln_conv_b, w_conv_out, w_att_out, w_o,
           g_ffn, w_ada_ffn, b_ada_ffn, w_router, b_router, w_exp_gate, w_exp_up, w_exp_down,
           w_sh_gate, w_sh_up, w_sh_down, g_final):
    B, S, _ = x_prompt.shape
    ns, T, _ = x_sample.shape
    assert g_mix.shape[0] == 1 and T == 1
    tm = min(TOKEN_TILE, S)
    span = DILATIONS[2] * 16
    assert S % tm == 0 and tm % span == 0 and S % (DILATIONS[2] * Q_BLOCK) == 0
    assert ns % SC_WINDOW == 0 and (B * S) % ns == 0
    ntok = B * S + ns
    caches = (cache_kv_w128, cache_kv_w512, cache_kv_w2048)

    row = lambda v: v.reshape(1, -1)
    w_in_b = w_in[0].astype(BF16)
    w_co_b = w_conv_out[0].astype(BF16)
    w_ao_b = w_att_out[0].astype(BF16)
    w_o_b = w_o[0].astype(BF16)
    wsg_b = w_sh_gate[0].astype(BF16)
    wsu_b = w_sh_up[0].astype(BF16)
    wsd_b = w_sh_down[0].astype(BF16)

    n_c = B + ns
    c_all = jnp.concatenate([c_prompt, c_sample], axis=0)
    c_all = jnp.pad(c_all, ((0, -n_c % 8), (0, 0)))
    mod_mix = _ada(c_all, w_ada_mix[0], b_ada_mix[0])
    mod_ffn = _ada(c_all, w_ada_ffn[0], b_ada_ffn[0])
    modm_p = mod_mix[:B].reshape(B, 1, 3 * D_MODEL)
    modf_p = mod_ffn[:B].reshape(B, 1, 3 * D_MODEL)
    modm_s = mod_mix[B:n_c].reshape(1, ns, 3 * D_MODEL)
    modf_s = mod_ffn[B:n_c].reshape(1, ns, 3 * D_MODEL)

    tables = _rope_parts(S, tm)
    (q0, k0, v0, q1, k1, v1, q2, k2, v2, gc_p, ga_p, utail) = _inproj_prompt(
        x_prompt, modm_p, row(g_mix[0]), w_in_b, tables, w_dw[0], row(b_dw[0]),
        row(ln_conv_g[0]), row(ln_conv_b[0]), w_co_b, tm)
    conv_prompt = utail[:, CONV_HALO - (CONV_WIDTH - 1):][None]

    attn_in = []
    lse_in = []
    kv_prompt = []
    for (qg, kg, vg), d in zip(((q0, k0, v0), (q1, k1, v1), (q2, k2, v2)), DILATIONS):
        l = S // d
        flat = lambda t: t.reshape(B * d, l, D_SLOT)
        o, lse = _attn_prompt(flat(qg), flat(kg), flat(vg))
        if d == 1:
            attn_in.append(o.reshape(B, S, D_SLOT))
            lse_in.append(lse.reshape(B, S, LANES))
        else:
            attn_in.append(o.reshape(B, d, l, D_SLOT))
            lse_in.append(lse.reshape(B, d, l, LANES))
        keep = min(Q_BLOCK * d, S)
        kv_prompt.append(_kv_tail(kg.reshape(B, d, l, D_SLOT), vg.reshape(B, d, l, D_SLOT), keep))

    wr_t = w_router[0].T
    wr_hi = wr_t.astype(BF16)
    wr_lo = (wr_t - wr_hi.astype(F32)).astype(BF16)
    b_col = b_router[0].reshape(N_EXPERTS, 1)
    zero_carry = jnp.zeros((N_EXPERTS, 1), F32)
    h2_p, x2_p, idx_p, wts_p, pos_p, cnt_p = _post(
        x_prompt, modm_p, (*attn_in, *lse_in), gc_p, ga_p, w_ao_b, w_o_b, row(g_ffn[0]), modf_p,
        wr_hi, wr_lo, b_col, wsg_b, wsu_b, wsd_b, zero_carry, tm, True)

    cos_s, sin_s = _rope_tables(jnp.full((1,), PAST_LEN, I32))
    u_s, q_s, k_s, v_s, sgc_s, ga_s = _inproj_sample(
        x_sample.reshape(ns, D_MODEL), mod_mix[B:n_c], row(g_mix[0]), w_in_b, cos_s, sin_s)
    heads = lambda t: t.reshape(ns, N_GROUPS, 1, HEADS, HEAD_DIM)
    qkv_rows = jnp.concatenate([heads(q_s), heads(k_s), heads(v_s)], axis=2)
    kv_sample, outs_s, lses_s = [], [], []
    for gi, (cache, d) in enumerate(zip(caches, DILATIONS)):
        cache_t = jnp.transpose(cache[0], (0, 2, 3, 4, 1))
        new_t, o, lse = _sample_cache(cache_t, qkv_rows, gi, d)
        kv_sample.append(jnp.transpose(new_t, (0, 4, 1, 2, 3))[None])
        outs_s.append(o.reshape(ns, D_SLOT))
        lses_s.append(lse.reshape(ns, D_SLOT))
    state_t = jnp.transpose(state_conv[0], (1, 0, 2))
    gc_s, att_s, cst_t = _sample_mix(state_t, u_s, sgc_s, w_dw[0], row(b_dw[0]), row(ln_conv_g[0]),
                                     row(ln_conv_b[0]), w_co_b, outs_s, lses_s)
    conv_sample = jnp.transpose(cst_t, (1, 0, 2))[None]
    as3 = lambda t: t.reshape(1, ns, t.shape[-1])
    h2_s, x2_s, idx_s, wts_s, pos_s, cnt = _post(
        as3(x_sample.reshape(ns, D_MODEL)), modm_s, (as3(att_s),), as3(gc_s), as3(ga_s), w_ao_b, w_o_b,
        row(g_ffn[0]), modf_s, wr_hi, wr_lo, b_col, wsg_b, wsu_b, wsd_b, cnt_p, ns, False)

    bm = EXPERT_BLOCK
    nblk = (ntok * TOP_K + N_EXPERTS * (bm - 1)) // bm
    counts = cnt[:, 0].astype(I32)
    padded = (counts + bm - 1) // bm * bm
    pend = jnp.cumsum(padded)
    pstart = pend - padded
    idx_all = jnp.concatenate([idx_p, idx_s], axis=1)
    pos_all = jnp.concatenate([pos_p, pos_s], axis=1)
    dest = _dest(pstart, idx_all, pos_all)
    dest_blocks = dest.reshape(TOP_K, ntok // SC_WINDOW, SC_WINDOW).transpose(1, 0, 2)
    wts_tok = jnp.concatenate([wts_p, wts_s], axis=1).T
    blk_row0 = jnp.arange(nblk, dtype=I32) * bm
    blk_e = jnp.minimum(jnp.sum((pend[None, :] <= blk_row0[:, None]).astype(I32), axis=1), N_EXPERTS - 1)
    mine = blk_e[:, None] == jnp.arange(N_EXPERTS, dtype=I32)[None, :]
    cnt_b = jnp.sum(jnp.where(mine, counts[None, :], 0), axis=1)
    start_b = jnp.sum(jnp.where(mine, pstart[None, :], 0), axis=1)
    blk_valid = jnp.clip(cnt_b - (blk_row0 - start_b), 0, bm).astype(I32)
    n_used = (pend[N_EXPERTS - 1] // bm).reshape(1)

    x_sorted = _dispatch_rows(h2_p, h2_s, dest_blocks, nblk * bm)
    y_sorted = _experts(blk_e, blk_valid, n_used, x_sorted, w_exp_gate[0], w_exp_up[0], w_exp_down[0], bm)
    yg = _gather_rows(y_sorted, dest_blocks)

    y_prompt = _final(x2_p, modf_p, wts_tok, yg, row(g_final), 0, tm)
    y_sample = _final(x2_s, modf_s, wts_tok, yg, row(g_final), (B * S) // ns, ns).reshape(ns, 1, D_MODEL)

    return (y_prompt, y_sample, kv_prompt[0], kv_prompt[1], kv_prompt[2], conv_prompt,
            kv_sample[0], kv_sample[1], kv_sample[2], conv_sample)
```

```python
import functools

import jax
import jax.numpy as jnp
from jax import lax
from jax.experimental import pallas as pl
from jax.experimental.pallas import tpu as pltpu
from jax.experimental.pallas import tpu_sc as plsc

F32 = jnp.float32
BF16 = jnp.bfloat16
I32 = jnp.int32

D_MODEL = 1024
D_CONV = 1024
CONV_WIDTH = 31
HEAD_DIM = 64
HEADS = 8
D_SLOT = HEADS * HEAD_DIM
DILATIONS = (1, 4, 16)
N_GROUPS = 3
D_ATT = N_GROUPS * D_SLOT
Q_BLOCK = 128
ROPE_THETA = 10000.0
N_EXPERTS = 256
TOP_K = 8
N_EXPERT_GROUPS = 8
TOPK_GROUPS = 4
GROUP_SIZE = N_EXPERTS // N_EXPERT_GROUPS
D_EXPERT = 256
ROUTED_SCALE = 2.5
RMS_EPS = 1e-6
LN_EPS = 1e-5

COL_A = 0
COL_B = D_CONV
COL_Q = 2 * D_CONV
COL_K = COL_Q + D_ATT
COL_V = COL_K + D_ATT
COL_GC = COL_V + D_ATT
COL_GA = COL_GC + D_MODEL
D_IN = COL_GA + D_MODEL

LANES = 128
CONV_HALO = 32
TOKEN_TILE = 512
EXPERT_BLOCK = 1152
EXPERT_ROW_STEPS = (128, 1024, 1152)
ROW_CHUNKS = 2
CHUNK = D_MODEL // 2 // ROW_CHUNKS
SC_WORKERS = 32
SC_WINDOW = 128
NEG_BIG = -1e30
VMEM_LIMIT = 56 * 1024 * 1024


def _sigmoid(x):
    return 1.0 / (1.0 + jnp.exp(-x))


def _const_spec(shape):
    nd = len(shape)
    return pl.BlockSpec(shape, lambda *_: (0,) * nd, pipeline_mode=pl.Buffered(1))


def _params(sem):
    return pltpu.CompilerParams(dimension_semantics=sem, vmem_limit_bytes=VMEM_LIMIT)


def _ada_kernel(c_ref, w_ref, b_ref, o_ref):
    c = c_ref[...]
    s = (c * _sigmoid(c)).astype(BF16)
    o_ref[...] = jnp.dot(s, w_ref[...].astype(BF16), preferred_element_type=F32) + b_ref[...]


def _ada(c_all, w, b):
    rows = c_all.shape[0]
    cols = w.shape[1]
    tn = 768
    return pl.pallas_call(
        _ada_kernel,
        grid=(cols // tn,),
        in_specs=[pl.BlockSpec((rows, D_MODEL), lambda j: (0, 0)),
                  pl.BlockSpec((D_MODEL, tn), lambda j: (0, j)),
                  pl.BlockSpec((1, tn), lambda j: (0, j))],
        out_specs=pl.BlockSpec((rows, tn), lambda j: (0, j)),
        out_shape=jax.ShapeDtypeStruct((rows, cols), F32),
        compiler_params=_params(("arbitrary",)),
        name="ada",
    )(c_all, w, b.reshape(1, cols))


def _modulated_norm(x, g, mod):
    shift = mod[:, 0:D_MODEL]
    scale = mod[:, D_MODEL:2 * D_MODEL]
    ms = jnp.mean(x * x, axis=-1, keepdims=True)
    return (x * lax.rsqrt(ms + RMS_EPS)) * g * (1.0 + scale) + shift


def _rope(t, cos, sin_signed):
    lane = lax.broadcasted_iota(I32, (t.shape[0], LANES), 1)
    first_half = (lane & (HEAD_DIM - 1)) < (HEAD_DIM // 2)
    outs = []
    for j in range(D_SLOT // LANES):
        ch = t[:, j * LANES:(j + 1) * LANES]
        rot = jnp.where(first_half, pltpu.roll(ch, LANES - HEAD_DIM // 2, 1), pltpu.roll(ch, HEAD_DIM // 2, 1))
        outs.append(ch * cos + rot * sin_signed)
    return jnp.concatenate(outs, axis=1)


def _layer_norm_swish(y, g, b):
    mu = jnp.mean(y, axis=-1, keepdims=True)
    yc = y - mu
    var = jnp.mean(yc * yc, axis=-1, keepdims=True)
    z = yc * lax.rsqrt(var + LN_EPS) * g + b
    return z * _sigmoid(z)


def _inproj_kernel(x_ref, mod_ref, g_ref, w_ref, base_ref, c0_ref, s0_ref, c1_ref, s1_ref, c2_ref, s2_ref,
                   wdw_ref, bdw_ref, lng_ref, lnb_ref, wco_ref,
                   q0_ref, k0_ref, v0_ref, q1_ref, k1_ref, v1_ref, q2_ref, k2_ref, v2_ref,
                   gc_ref, ga_ref, ut_ref,
                   hf_ref, hb0_ref, hb1_ref, hb2_ref, ubuf_ref, *, tm):
    i = pl.program_id(1)
    cw = 512

    h = _modulated_norm(x_ref[...], g_ref[...], mod_ref[...])
    nlc = D_MODEL // LANES
    hb0_ref[...] = h.astype(BF16)

    def mm(hb, col, width=cw):
        return jnp.dot(hb[...], w_ref[:, col:col + width], preferred_element_type=F32)

    @pl.when(i == 0)
    def _():
        ubuf_ref[:, 0:CONV_HALO, :] = jnp.zeros((nlc, CONV_HALO, LANES), F32)

    for c in range(0, D_CONV, cw):
        a = mm(hb0_ref, COL_A + c)
        b = mm(hb0_ref, COL_B + c)
        u = a * _sigmoid(b)
        for cc in range(cw // LANES):
            ubuf_ref[c // LANES + cc, CONV_HALO:CONV_HALO + tm, :] = u[:, cc * LANES:(cc + 1) * LANES]

    for c in range(nlc):
        hf_ref[c] = h[:, c * LANES:(c + 1) * LANES]
    for d, hb in ((DILATIONS[1], hb1_ref), (DILATIONS[2], hb2_ref)):
        n = tm // d
        for r in range(d):
            for c in range(nlc):
                hb[r * n:(r + 1) * n, c * LANES:(c + 1) * LANES] = hf_ref[c, pl.ds(r, n, stride=d), :].astype(BF16)

    groups = ((hb0_ref, 1, c0_ref, s0_ref, q0_ref, k0_ref, v0_ref),
              (hb1_ref, DILATIONS[1], c1_ref, s1_ref, q1_ref, k1_ref, v1_ref),
              (hb2_ref, DILATIONS[2], c2_ref, s2_ref, q2_ref, k2_ref, v2_ref))
    cos_b = base_ref[0:1, :]
    sin_b = base_ref[1:2, :]
    lane = lax.broadcasted_iota(I32, (1, LANES), 1)
    sign = jnp.where((lane & (HEAD_DIM - 1)) < (HEAD_DIM // 2), -1.0, 1.0)
    for gi, (hb, d, c_ref, s_ref, q_ref, k_ref, v_ref) in enumerate(groups):
        cos_w = c_ref[...]
        sin_w = s_ref[...]
        cos = cos_b * cos_w - sin_b * sin_w
        sin = (sin_b * cos_w + cos_b * sin_w) * sign
        n = tm // d

        def put(ref, val):
            vb = val.astype(BF16)
            if d == 1:
                ref[...] = vb
            else:
                for r in range(d):
                    ref[r] = vb[r * n:(r + 1) * n, :]

        put(q_ref, _rope(mm(hb, COL_Q + gi * D_SLOT), cos, sin) * (HEAD_DIM ** -0.5))
        put(k_ref, _rope(mm(hb, COL_K + gi * D_SLOT), cos, sin))
        put(v_ref, mm(hb, COL_V + gi * D_SLOT))

    rb = 32
    first_tap = CONV_HALO - (CONV_WIDTH - 1)

    def conv_rows(r, carry):
        r0 = pl.multiple_of(r * rb, rb)
        for c in range(nlc):
            cs = slice(c * LANES, (c + 1) * LANES)
            acc = jnp.broadcast_to(bdw_ref[:, cs], (rb, LANES))
            for k in range(CONV_WIDTH):
                acc = acc + wdw_ref[k:k + 1, cs] * ubuf_ref[c, pl.ds(r0 + (first_tap + k), rb), :]
            hf_ref[c, pl.ds(r0, rb), :] = acc
        return carry

    lax.fori_loop(0, tm // rb, conv_rows, 0)

    for c in range(0, D_MODEL, cw):
        ga_ref[:, c:c + cw] = _sigmoid(mm(hb0_ref, COL_GA + c)).astype(BF16)
    y_dw = jnp.concatenate([hf_ref[c] for c in range(nlc)], axis=1)
    z = _layer_norm_swish(y_dw, lng_ref[...], lnb_ref[...]).astype(BF16)
    for c in range(0, D_MODEL, cw):
        co = jnp.dot(z, wco_ref[:, c:c + cw], preferred_element_type=F32)
        gc_ref[:, c:c + cw] = (_sigmoid(mm(hb0_ref, COL_GC + c)) * co).astype(BF16)

    for c in range(nlc):
        tail = ubuf_ref[c, tm:tm + CONV_HALO, :]
        ut_ref[:, c * LANES:(c + 1) * LANES] = tail
        ubuf_ref[c, 0:CONV_HALO, :] = tail


def _inproj_prompt(x, mod, g, w_in_b, tables, w_dw, b_dw, ln_g, ln_b, w_co_b, tm):
    B, S, _ = x.shape
    nt = S // tm
    d1, d2 = DILATIONS[1], DILATIONS[2]
    tok = lambda b, i: (b, i, 0)
    tab = _const_spec((tm, LANES))
    in_specs = [
        pl.BlockSpec((None, tm, D_MODEL), tok),
        pl.BlockSpec((None, 1, 3 * D_MODEL), lambda b, i: (b, 0, 0)),
        _const_spec((1, D_MODEL)),
        _const_spec((D_MODEL, D_IN)),
        pl.BlockSpec((None, 2, LANES), lambda b, i: (i, 0, 0)),
        tab, tab, tab, tab, tab, tab,
        _const_spec((CONV_WIDTH, D_CONV)),
        _const_spec((1, D_CONV)),
        _const_spec((1, D_CONV)),
        _const_spec((1, D_CONV)),
        _const_spec((D_CONV, D_MODEL)),
    ]
    nat = pl.BlockSpec((None, tm, D_SLOT), tok)
    st1 = pl.BlockSpec((None, d1, tm // d1, D_SLOT), lambda b, i: (b, 0, i, 0))
    st2 = pl.BlockSpec((None, d2, tm // d2, D_SLOT), lambda b, i: (b, 0, i, 0))
    out_specs = [nat, nat, nat, st1, st1, st1, st2, st2, st2,
                 pl.BlockSpec((None, tm, D_MODEL), tok),
                 pl.BlockSpec((None, tm, D_MODEL), tok),
                 pl.BlockSpec((None, CONV_HALO, D_CONV), lambda b, i: (b, 0, 0))]
    s0 = jax.ShapeDtypeStruct((B, S, D_SLOT), BF16)
    s1 = jax.ShapeDtypeStruct((B, d1, S // d1, D_SLOT), BF16)
    s2 = jax.ShapeDtypeStruct((B, d2, S // d2, D_SLOT), BF16)
    out_shape = [s0, s0, s0, s1, s1, s1, s2, s2, s2,
                 jax.ShapeDtypeStruct((B, S, D_MODEL), BF16),
                 jax.ShapeDtypeStruct((B, S, D_MODEL), BF16),
                 jax.ShapeDtypeStruct((B, CONV_HALO, D_CONV), F32)]
    scratch = [pltpu.VMEM((D_MODEL // LANES, tm, LANES), F32),
               pltpu.VMEM((tm, D_MODEL), BF16),
               pltpu.VMEM((tm, D_MODEL), BF16),
               pltpu.VMEM((tm, D_MODEL), BF16),
               pltpu.VMEM((D_CONV // LANES, tm + CONV_HALO, LANES), F32)]
    return pl.pallas_call(
        functools.partial(_inproj_kernel, tm=tm),
        grid=(B, nt),
        in_specs=in_specs, out_specs=out_specs, out_shape=out_shape,
        scratch_shapes=scratch,
        compiler_params=_params(("arbitrary", "arbitrary")),
        name="inproj",
    )(x, mod, g, w_in_b, *tables, w_dw, b_dw, ln_g, ln_b, w_co_b)


def _attn_kernel(q_ref, kp_ref, kc_ref, vp_ref, vc_ref, o_ref, lse_ref, k_all, v_all, *, nq):
    j = pl.program_id(1)
    k_all[0:Q_BLOCK, :] = kp_ref[...]
    k_all[Q_BLOCK:(nq + 1) * Q_BLOCK, :] = kc_ref[...]
    v_all[0:Q_BLOCK, :] = vp_ref[...]
    v_all[Q_BLOCK:(nq + 1) * Q_BLOCK, :] = vc_ref[...]
    qi = lax.broadcasted_iota(I32, (Q_BLOCK, 2 * Q_BLOCK), 0)
    ki = lax.broadcasted_iota(I32, (Q_BLOCK, 2 * Q_BLOCK), 1)
    back = Q_BLOCK + qi - ki
    in_band = jnp.logical_and(back >= 0, back <= Q_BLOCK)
    first_mask = jnp.logical_and(in_band, jnp.logical_or(ki >= Q_BLOCK, j > 0))
    lane = lax.broadcasted_iota(I32, (Q_BLOCK, LANES), 1)
    even = lax.broadcasted_iota(I32, (1, LANES), 1) < HEAD_DIM
    zero = jnp.zeros((), BF16)
    nt = (((1,), (1,)), ((), ()))
    for sub in range(nq):
        mask = first_mask if sub == 0 else in_band
        qrows = slice(sub * Q_BLOCK, (sub + 1) * Q_BLOCK)
        krows = slice(sub * Q_BLOCK, (sub + 2) * Q_BLOCK)
        scores = []
        for h in range(HEADS):
            pair = slice((h // 2) * LANES, (h // 2 + 1) * LANES)
            mine = even if h % 2 == 0 else jnp.logical_not(even)
            qh = jnp.where(mine, q_ref[qrows, pair], zero)
            scores.append(lax.dot_general(qh, k_all[krows, pair], nt, preferred_element_type=F32))
        probs, dens = [], []
        lse_all = jnp.zeros((Q_BLOCK, LANES), F32)
        for h in range(HEADS):
            s = jnp.where(mask, scores[h], NEG_BIG)
            m = jnp.max(s, axis=-1, keepdims=True)
            p = jnp.exp(s - m)
            den = jnp.sum(p, axis=-1, keepdims=True)
            probs.append(p.astype(BF16))
            dens.append(den)
            lse_all = jnp.where(lane == h, m + jnp.log(den), lse_all)
        for hp in range(HEADS // 2):
            pair = slice(hp * LANES, (hp + 1) * LANES)
            v2 = v_all[krows, pair]
            oe = jnp.dot(probs[2 * hp], v2, preferred_element_type=F32) / dens[2 * hp]
            oo = jnp.dot(probs[2 * hp + 1], v2, preferred_element_type=F32) / dens[2 * hp + 1]
            o_ref[qrows, pair] = jnp.where(even, oe, oo).astype(BF16)
        lse_ref[qrows, :] = lse_all


def _attn_prompt(q, k, v):
    ns, L, _ = q.shape
    nq = max(c for c in (4, 2, 1) if (L // Q_BLOCK) % c == 0)
    rows = nq * Q_BLOCK
    cur = pl.BlockSpec((None, rows, D_SLOT), lambda n, j: (n, j, 0))
    prev = pl.BlockSpec((None, Q_BLOCK, D_SLOT), lambda n, j: (n, jnp.maximum(j * nq - 1, 0), 0))
    return pl.pallas_call(
        functools.partial(_attn_kernel, nq=nq),
        grid=(ns, L // rows),
        in_specs=[cur, prev, cur, prev, cur],
        out_specs=[cur, pl.BlockSpec((None, rows, LANES), lambda n, j: (n, j, 0))],
        out_shape=[jax.ShapeDtypeStruct((ns, L, D_SLOT), BF16),
                   jax.ShapeDtypeStruct((ns, L, LANES), F32)],
        scratch_shapes=[pltpu.VMEM((rows + Q_BLOCK, D_SLOT), BF16), pltpu.VMEM((rows + Q_BLOCK, D_SLOT), BF16)],
        compiler_params=_params(("arbitrary", "arbitrary")),
        name="attn",
    )(q, k, k, v, v)


def _route_t(logits_t, b_col, carry, tm):
    ninf = -jnp.inf
    scores = _sigmoid(logits_t)
    sel = scores + b_col
    rowf = lax.broadcasted_iota(I32, (N_EXPERTS, tm), 0).astype(F32)
    past_end = float(N_EXPERTS)

    def first_max(x, rows):
        m = jnp.max(x, axis=0, keepdims=True)
        return m, jnp.min(jnp.where(x == m, rows, past_end), axis=0, keepdims=True)

    gs = []
    rowg = lax.broadcasted_iota(I32, (GROUP_SIZE, tm), 0).astype(F32)
    for g in range(N_EXPERT_GROUPS):
        rs = slice(g * GROUP_SIZE, (g + 1) * GROUP_SIZE)
        m1, i1 = first_max(sel[rs], rowg)
        m2 = jnp.max(jnp.where(rowg == i1, ninf, sel[rs]), axis=0, keepdims=True)
        gs.append(m1 + m2)
    pieces = []
    for g in range(N_EXPERT_GROUPS):
        beaten = jnp.zeros((1, tm), F32)
        for g2 in range(N_EXPERT_GROUPS):
            if g2 != g:
                better = gs[g2] >= gs[g] if g2 < g else gs[g2] > gs[g]
                beaten = beaten + jnp.where(better, 1.0, 0.0)
        rs = slice(g * GROUP_SIZE, (g + 1) * GROUP_SIZE)
        pieces.append(jnp.where(beaten < TOPK_GROUPS, sel[rs], ninf))
    selm = jnp.concatenate(pieces, axis=0)

    idx_rows, w_rows = [], []
    picked = jnp.zeros((N_EXPERTS, tm), F32)
    for _ in range(TOP_K):
        _, ik = first_max(selm, rowf)
        hit = rowf == ik
        w_rows.append(jnp.sum(jnp.where(hit, scores, 0.0), axis=0, keepdims=True))
        selm = jnp.where(hit, ninf, selm)
        picked = jnp.where(hit, 1.0, picked)
        idx_rows.append(ik)
    wsum = w_rows[0]
    for wk in w_rows[1:]:
        wsum = wsum + wk
    denom = wsum + 1e-20

    ti = lax.broadcasted_iota(I32, (tm, tm), 0)
    tj = lax.broadcasted_iota(I32, (tm, tm), 1)
    earlier = jnp.where(ti < tj, 1.0, 0.0).astype(BF16)
    before = jnp.dot(picked.astype(BF16), earlier, preferred_element_type=F32) + carry
    new_carry = carry + jnp.sum(picked, axis=1, keepdims=True)
    pos_rows = [jnp.sum(jnp.where(rowf == ik, before, 0.0), axis=0, keepdims=True) for ik in idx_rows]

    idx_o = jnp.concatenate(idx_rows, axis=0).astype(I32)
    w_o = jnp.concatenate([wk / denom * ROUTED_SCALE for wk in w_rows], axis=0)
    pos_o = jnp.concatenate(pos_rows, axis=0).astype(I32)
    return idx_o, w_o, pos_o, new_carry


def _pack_rows(xb):
    bits = lax.bitcast_convert_type(xb.astype(F32), jnp.uint32)
    half = D_MODEL // 2
    word = bits[:, half:] | (bits[:, :half] >> 16)
    return lax.bitcast_convert_type(word, F32)


def _unpack_rows(words):
    bits = lax.bitcast_convert_type(words, jnp.uint32)
    lo = lax.bitcast_convert_type(bits << 16, F32)
    hi = lax.bitcast_convert_type(bits & jnp.uint32(0xFFFF0000), F32)
    return jnp.concatenate([lo, hi], axis=1)


def _post_kernel(*refs, tm, combine):
    refs = list(refs)
    x_ref, modm_ref = refs[0:2]
    p = 2
    if combine:
        o0_ref, o1_ref, o2_ref, l0_ref, l1_ref, l2_ref = refs[p:p + 6]
        p += 6
    else:
        att_ref = refs[p]
        p += 1
    (gc_ref, ga_ref, wao_ref, wo_ref, gffn_ref, modf_ref, wrh_ref, wrl_ref, br_ref,
     wsg_ref, wsu_ref, wsd_ref, cin_ref) = refs[p:p + 13]
    p += 13
    h2_ref, x2_ref, idx_ref, wts_ref, pos_ref, cnt_ref = refs[p:p + 6]
    p += 6
    if combine:
        o1n_ref, o2n_ref, l1n_ref, l2n_ref = refs[p:p + 4]

    first = jnp.logical_and(pl.program_id(0) == 0, pl.program_id(1) == 0)

    @pl.when(first)
    def _():
        cnt_ref[...] = cin_ref[...]

    if combine:
        nsc = D_SLOT // LANES
        for d, o_ref, l_ref, on_ref, ln_ref in ((DILATIONS[1], o1_ref, l1_ref, o1n_ref, l1n_ref),
                                                (DILATIONS[2], o2_ref, l2_ref, o2n_ref, l2n_ref)):
            n = tm // d
            for r in range(d):
                ln_ref[pl.ds(r, n, stride=d), :] = l_ref[r]
                orow = o_ref[r].astype(F32)
                for c in range(nsc):
                    on_ref[c, pl.ds(r, n, stride=d), :] = orow[:, c * LANES:(c + 1) * LANES]
        l0 = l0_ref[...]
        l1 = l1n_ref[...]
        l2 = l2n_ref[...]
        mx = jnp.maximum(jnp.maximum(l0, l1), l2)
        e0 = jnp.exp(l0 - mx)
        e1 = jnp.exp(l1 - mx)
        e2 = jnp.exp(l2 - mx)
        esum = e0 + e1 + e2
        even = lax.broadcasted_iota(I32, (tm, LANES), 1) < HEAD_DIM

        def expand(w):
            wide = lambda h: jnp.broadcast_to(w[:, h:h + 1], (tm, LANES))
            return jnp.concatenate([jnp.where(even, wide(2 * c), wide(2 * c + 1)) for c in range(nsc)], axis=1)

        o1n = jnp.concatenate([o1n_ref[c] for c in range(nsc)], axis=1)
        o2n = jnp.concatenate([o2n_ref[c] for c in range(nsc)], axis=1)
        att = (expand(e0 / esum) * o0_ref[...].astype(F32) + expand(e1 / esum) * o1n
               + expand(e2 / esum) * o2n)
    else:
        att = att_ref[...]

    att_out = jnp.dot(att.astype(BF16), wao_ref[...], preferred_element_type=F32)
    merged = gc_ref[...] + ga_ref[...] * att_out.astype(BF16)
    y = jnp.dot(merged, wo_ref[...], preferred_element_type=F32)
    x1 = x_ref[...] + modm_ref[:, 2 * D_MODEL:3 * D_MODEL] * y

    modf = modf_ref[...]
    h2 = _modulated_norm(x1, gffn_ref[...], modf)
    h2b = h2.astype(BF16)
    words = _pack_rows(h2b)
    for c in range(ROW_CHUNKS):
        h2_ref[c] = words[:, c * CHUNK:(c + 1) * CHUNK]
    sg = jnp.dot(h2b, wsg_ref[...], preferred_element_type=F32)
    su = jnp.dot(h2b, wsu_ref[...], preferred_element_type=F32)
    sh = jnp.dot((sg * _sigmoid(sg) * su).astype(BF16), wsd_ref[...], preferred_element_type=F32)
    x2_ref[...] = x1 + modf[:, 2 * D_MODEL:3 * D_MODEL] * sh

    h2l = (h2 - h2b.astype(F32)).astype(BF16)
    nt = (((1,), (1,)), ((), ()))
    logits_t = (lax.dot_general(wrh_ref[...], h2b, nt, preferred_element_type=F32)
                + lax.dot_general(wrl_ref[...], h2b, nt, preferred_element_type=F32)
                + lax.dot_general(wrh_ref[...], h2l, nt, preferred_element_type=F32))
    carry = cnt_ref[...]
    for lc in range(tm // LANES):
        ls = slice(lc * LANES, (lc + 1) * LANES)
        idx_o, w_o, pos_o, carry = _route_t(logits_t[:, ls], br_ref[...], carry, LANES)
        idx_ref[:, ls] = idx_o
        wts_ref[:, ls] = w_o
        pos_ref[:, ls] = pos_o
    cnt_ref[...] = carry


def _post(x, modm, attn_inputs, gc, ga, w_ao_b, w_o_b, g_ffn, modf, wr_hi, wr_lo, b_router,
          wsg_b, wsu_b, wsd_b, carry_in, tm, combine):
    B, S, _ = x.shape
    nt = S // tm
    tok = lambda b, i: (b, i, 0)
    stream = lambda b, i: (b, 0, i, 0)
    mod_rows = modm.shape[1]
    mod_spec = (pl.BlockSpec((None, 1, 3 * D_MODEL), lambda b, i: (b, 0, 0)) if mod_rows == 1
                else pl.BlockSpec((None, tm, 3 * D_MODEL), tok))
    full = pl.BlockSpec((None, tm, D_MODEL), tok)
    in_specs = [full, mod_spec]
    scratch = []
    if combine:
        d1, d2 = DILATIONS[1], DILATIONS[2]
        in_specs += [pl.BlockSpec((None, tm, D_SLOT), tok),
                     pl.BlockSpec((None, d1, tm // d1, D_SLOT), stream),
                     pl.BlockSpec((None, d2, tm // d2, D_SLOT), stream),
                     pl.BlockSpec((None, tm, LANES), tok),
                     pl.BlockSpec((None, d1, tm // d1, LANES), stream),
                     pl.BlockSpec((None, d2, tm // d2, LANES), stream)]
        scratch += [pltpu.VMEM((D_SLOT // LANES, tm, LANES), F32),
                    pltpu.VMEM((D_SLOT // LANES, tm, LANES), F32),
                    pltpu.VMEM((tm, LANES), F32),
                    pltpu.VMEM((tm, LANES), F32)]
    else:
        in_specs += [pl.BlockSpec((None, tm, D_SLOT), tok)]
    in_specs += [full, full,
                 _const_spec((D_SLOT, D_MODEL)), _const_spec((D_MODEL, D_MODEL)),
                 _const_spec((1, D_MODEL)), mod_spec,
                 _const_spec((N_EXPERTS, D_MODEL)), _const_spec((N_EXPERTS, D_MODEL)),
                 _const_spec((N_EXPERTS, 1)),
                 _const_spec((D_MODEL, D_EXPERT)), _const_spec((D_MODEL, D_EXPERT)),
                 _const_spec((D_EXPERT, D_MODEL)), _const_spec((N_EXPERTS, 1))]
    pack = pl.BlockSpec((TOP_K, tm), lambda b, i: (0, b * nt + i))
    chunked = pl.BlockSpec((ROW_CHUNKS, tm, CHUNK), lambda b, i: (0, b * nt + i, 0))
    out_specs = [chunked, full, pack, pack, pack, pl.BlockSpec((N_EXPERTS, 1), lambda b, i: (0, 0))]
    out_shape = [jax.ShapeDtypeStruct((ROW_CHUNKS, B * S, CHUNK), F32),
                 jax.ShapeDtypeStruct((B, S, D_MODEL), F32),
                 jax.ShapeDtypeStruct((TOP_K, B * S), I32),
                 jax.ShapeDtypeStruct((TOP_K, B * S), F32),
                 jax.ShapeDtypeStruct((TOP_K, B * S), I32),
                 jax.ShapeDtypeStruct((N_EXPERTS, 1), F32)]
    return pl.pallas_call(
        functools.partial(_post_kernel, tm=tm, combine=combine),
        grid=(B, nt),
        in_specs=in_specs, out_specs=out_specs, out_shape=out_shape,
        scratch_shapes=scratch,
        compiler_params=_params(("arbitrary", "arbitrary")),
        name="post_prompt" if combine else "post_sample",
    )(x, modm, *attn_inputs, gc, ga, w_ao_b, w_o_b, g_ffn, modf, wr_hi, wr_lo, b_router,
      wsg_b, wsu_b, wsd_b, carry_in)


def _inproj_sample_kernel(x_ref, mod_ref, g_ref, w_ref, cos_ref, sin_ref,
                          u_ref, q_ref, k_ref, v_ref, sgc_ref, ga_ref):
    cw = 512
    hb = _modulated_norm(x_ref[...], g_ref[...], mod_ref[...]).astype(BF16)

    def mm(col, width=cw):
        return jnp.dot(hb, w_ref[:, col:col + width], preferred_element_type=F32)

    cos = cos_ref[...]
    sin = sin_ref[...]
    for c in range(0, D_CONV, cw):
        u_ref[:, c:c + cw] = mm(COL_A + c) * _sigmoid(mm(COL_B + c))
        sgc_ref[:, c:c + cw] = _sigmoid(mm(COL_GC + c))
        ga_ref[:, c:c + cw] = _sigmoid(mm(COL_GA + c)).astype(BF16)
    for gi in range(N_GROUPS):
        cs = slice(gi * D_SLOT, (gi + 1) * D_SLOT)
        q_ref[:, cs] = _rope(mm(COL_Q + gi * D_SLOT), cos, sin) * (HEAD_DIM ** -0.5)
        k_ref[:, cs] = _rope(mm(COL_K + gi * D_SLOT), cos, sin)
        v_ref[:, cs] = mm(COL_V + gi * D_SLOT)


def _inproj_sample(x, mod, g, w_in_b, cos, sin):
    ns = x.shape[0]
    whole = lambda shape: pl.BlockSpec(shape, lambda i: (0,) * len(shape))
    f = lambda cols, dt=F32: jax.ShapeDtypeStruct((ns, cols), dt)
    return pl.pallas_call(
        _inproj_sample_kernel,
        grid=(1,),
        in_specs=[whole((ns, D_MODEL)), whole((ns, 3 * D_MODEL)), _const_spec((1, D_MODEL)),
                  _const_spec((D_MODEL, D_IN)), whole((1, LANES)), whole((1, LANES))],
        out_specs=[whole((ns, D_CONV)), whole((ns, D_ATT)), whole((ns, D_ATT)), whole((ns, D_ATT)),
                   whole((ns, D_MODEL)), whole((ns, D_MODEL))],
        out_shape=[f(D_CONV), f(D_ATT), f(D_ATT), f(D_ATT), f(D_MODEL), f(D_MODEL, BF16)],
        compiler_params=_params(("arbitrary",)),
        name="inproj_sample",
    )(x, mod, g, w_in_b, cos, sin)


def _sample_mix_kernel(state_ref, u_ref, sgc_ref, wdw_ref, bdw_ref, lng_ref, lnb_ref, wco_ref,
                       o0_ref, o1_ref, o2_ref, l0_ref, l1_ref, l2_ref,
                       gc_ref, att_ref, cst_ref):
    nprev = CONV_WIDTH - 1
    u = u_ref[...]
    y = wdw_ref[nprev:CONV_WIDTH, :] * u + bdw_ref[...]
    for k in range(nprev):
        y = y + wdw_ref[k:k + 1, :] * state_ref[k]
    z = _layer_norm_swish(y, lng_ref[...], lnb_ref[...]).astype(BF16)
    co = jnp.dot(z, wco_ref[...], preferred_element_type=F32)
    gc_ref[...] = (sgc_ref[...] * co).astype(BF16)
    for k in range(nprev - 1):
        cst_ref[k] = state_ref[k + 1]
    cst_ref[nprev - 1] = u

    l0 = l0_ref[...]
    l1 = l1_ref[...]
    l2 = l2_ref[...]
    mx = jnp.maximum(jnp.maximum(l0, l1), l2)
    e0 = jnp.exp(l0 - mx)
    e1 = jnp.exp(l1 - mx)
    e2 = jnp.exp(l2 - mx)
    esum = e0 + e1 + e2
    att_ref[...] = (e0 / esum) * o0_ref[...] + (e1 / esum) * o1_ref[...] + (e2 / esum) * o2_ref[...]


def _sample_mix(state, u, sgc, w_dw, b_dw, ln_g, ln_b, w_co_b, outs, lses):
    ns = state.shape[1]
    nprev = CONV_WIDTH - 1
    whole = lambda *shape: pl.BlockSpec(shape, lambda i: (0,) * len(shape))
    slot = whole(ns, D_SLOT)
    return pl.pallas_call(
        _sample_mix_kernel,
        grid=(1,),
        in_specs=[whole(nprev, ns, D_CONV), whole(ns, D_CONV), whole(ns, D_MODEL),
                  whole(CONV_WIDTH, D_CONV), whole(1, D_CONV), whole(1, D_CONV), whole(1, D_CONV),
                  whole(D_CONV, D_MODEL), slot, slot, slot, slot, slot, slot],
        out_specs=[whole(ns, D_MODEL), slot, whole(nprev, ns, D_CONV)],
        out_shape=[jax.ShapeDtypeStruct((ns, D_MODEL), BF16),
                   jax.ShapeDtypeStruct((ns, D_SLOT), F32),
                   jax.ShapeDtypeStruct((nprev, ns, D_CONV), F32)],
        compiler_params=_params(("arbitrary",)),
        name="sample_mix",
    )(state, u, sgc, w_dw, b_dw, ln_g, ln_b, w_co_b, *outs, *lses)


def _split3(x):
    p1 = x.astype(BF16)
    r1 = x - p1.astype(F32)
    p2 = r1.astype(BF16)
    p3 = (r1 - p2.astype(F32)).astype(BF16)
    return p1, p2, p3


def _sample_cache_kernel(c_ref, r_ref, co_ref, o_ref, l_ref, *, n, dil, sb):
    lane = lax.broadcasted_iota(I32, (1, n), 1)
    back = n - lane
    use = jnp.logical_and(jnp.logical_and(back % dil == 0, back <= Q_BLOCK * dil), back >= dil)
    last = lax.broadcasted_iota(I32, (HEAD_DIM, n), 1) == n - 1
    nt = (((1,), (1,)), ((), ()))
    eye_d = (lax.broadcasted_iota(I32, (HEAD_DIM, HEAD_DIM), 0)
             == lax.broadcasted_iota(I32, (HEAD_DIM, HEAD_DIM), 1)).astype(BF16)
    kv_rows = r_ref[:, 1:3].reshape(sb * 2 * HEADS, HEAD_DIM)
    kv_cols = sum(lax.dot_general(eye_d, part, nt, preferred_element_type=F32) for part in _split3(kv_rows))

    for b in range(sb):
        q = r_ref[b, 0]
        qb = q.astype(BF16)
        rows = []
        for h in range(HEADS):
            kh = c_ref[b, 0, h].astype(BF16)
            rows.append(jnp.dot(qb, kh, preferred_element_type=F32)[h:h + 1, :])
        s = jnp.where(use, jnp.concatenate(rows, axis=0), NEG_BIG)
        sn = jnp.sum(q * r_ref[b, 1], axis=1, keepdims=True)
        m = jnp.maximum(jnp.max(s, axis=1, keepdims=True), sn)
        p = jnp.exp(s - m)
        pn = jnp.exp(sn - m)
        den = jnp.sum(p, axis=1, keepdims=True) + pn
        pb = p.astype(BF16)
        outs = []
        for h in range(HEADS):
            vh = c_ref[b, 1, h].astype(BF16)
            outs.append(lax.dot_general(pb, vh, nt, preferred_element_type=F32)[h:h + 1, :])
        o_ref[b] = (jnp.concatenate(outs, axis=0) + pn * r_ref[b, 2]) / den
        l_ref[b] = jnp.broadcast_to(m + jnp.log(den), (HEADS, HEAD_DIM))
        for h in range(HEADS):
            for kind in range(2):
                j = (b * 2 + kind) * HEADS + h
                co_ref[b, kind, h] = jnp.where(last, kv_cols[:, j:j + 1], pltpu.roll(c_ref[b, kind, h], n - 1, 1))


def _sample_cache(cache_t, qkv_rows, gi, dil):
    ns, _, _, _, n = cache_t.shape
    seq_bytes = 2 * HEADS * HEAD_DIM * n * 4
    sb = max(1, min(4, (4 * 1024 * 1024) // seq_bytes))
    assert ns % sb == 0
    cspec = pl.BlockSpec((sb, 2, HEADS, HEAD_DIM, n), lambda b: (b, 0, 0, 0, 0))
    rspec = pl.BlockSpec((sb, None, 3, HEADS, HEAD_DIM), lambda b: (b, gi, 0, 0, 0))
    ospec = pl.BlockSpec((sb, HEADS, HEAD_DIM), lambda b: (b, 0, 0))
    slot = jax.ShapeDtypeStruct((ns, HEADS, HEAD_DIM), F32)
    return pl.pallas_call(
        functools.partial(_sample_cache_kernel, n=n, dil=dil, sb=sb),
        grid=(ns // sb,),
        in_specs=[cspec, rspec],
        out_specs=[cspec, ospec, ospec],
        out_shape=[jax.ShapeDtypeStruct(cache_t.shape, F32), slot, slot],
        compiler_params=_params(("arbitrary",)),
        name="sample_cache",
    )(cache_t, qkv_rows)


def _experts_kernel(be_ref, bv_ref, nu_ref, x_ref, wg_ref, wu_ref, wd_ref, y_ref, wgb_ref, wub_ref, wdb_ref, *, bm):
    i = pl.program_id(0)
    e = be_ref[i]
    prev = be_ref[jnp.maximum(i - 1, 0)]
    valid = jnp.where(i < nu_ref[0], bv_ref[i], 0)

    @pl.when(jnp.logical_and(valid > 0, jnp.logical_or(i == 0, e != prev)))
    def _():
        wgb_ref[...] = wg_ref[...].astype(BF16)
        wub_ref[...] = wu_ref[...].astype(BF16)
        wdb_ref[...] = wd_ref[...].astype(BF16)

    def run(rows):
        words = jnp.concatenate([x_ref[c, 0:rows, :] for c in range(ROW_CHUNKS)], axis=1)
        row = lax.broadcasted_iota(I32, (rows, D_MODEL), 0)
        x = jnp.where(row < valid, _unpack_rows(words), 0.0).astype(BF16)
        g = jnp.dot(x, wgb_ref[...], preferred_element_type=F32)
        u = jnp.dot(x, wub_ref[...], preferred_element_type=F32)
        a = (g * _sigmoid(g) * u).astype(BF16)
        y = jnp.dot(a, wdb_ref[...], preferred_element_type=F32)
        yw = _pack_rows(y.astype(BF16))
        for c in range(ROW_CHUNKS):
            y_ref[c, 0:rows, :] = yw[:, c * CHUNK:(c + 1) * CHUNK]
            if rows < bm:
                y_ref[c, rows:bm, :] = jnp.zeros((bm - rows, CHUNK), F32)

    lower = 0
    for rows in EXPERT_ROW_STEPS:
        upper_ok = valid <= rows if rows < bm else True

        @pl.when(jnp.logical_and(valid > lower, upper_ok))
        def _(rows=rows):
            run(rows)

        lower = rows


def _experts(blk_e, blk_valid, n_used, x_sorted, w_g, w_u, w_d, bm):
    rows = x_sorted.shape[1]
    nblk = rows // bm
    blk = lambda i, be, bv, nu: (0, jnp.minimum(i, nu[0] - 1), 0)
    wsel = lambda i, be, bv, nu: (be[jnp.minimum(i, nu[0] - 1)], 0, 0)
    grid_spec = pltpu.PrefetchScalarGridSpec(
        num_scalar_prefetch=3,
        grid=(nblk,),
        in_specs=[pl.BlockSpec((ROW_CHUNKS, bm, CHUNK), blk),
                  pl.BlockSpec((None, D_MODEL, D_EXPERT), wsel),
                  pl.BlockSpec((None, D_MODEL, D_EXPERT), wsel),
                  pl.BlockSpec((None, D_EXPERT, D_MODEL), wsel)],
        out_specs=pl.BlockSpec((ROW_CHUNKS, bm, CHUNK), blk),
        scratch_shapes=[pltpu.VMEM((D_MODEL, D_EXPERT), BF16), pltpu.VMEM((D_MODEL, D_EXPERT), BF16),
                        pltpu.VMEM((D_EXPERT, D_MODEL), BF16)])
    return pl.pallas_call(
        functools.partial(_experts_kernel, bm=bm),
        grid_spec=grid_spec,
        out_shape=jax.ShapeDtypeStruct((ROW_CHUNKS, rows, CHUNK), F32),
        compiler_params=_params(("arbitrary",)),
        name="experts",
    )(blk_e, blk_valid, n_used, x_sorted, w_g, w_u, w_d)


def _final_kernel(x2_ref, modf_ref, wts_ref, yg_ref, gfin_ref, *rest, live_batches):
    o_ref = rest[-1]
    live = pl.program_id(0) < live_batches

    @pl.when(live)
    def _():
        w = wts_ref[...]

        def picked(k):
            return _unpack_rows(jnp.concatenate([yg_ref[k, c] for c in range(ROW_CHUNKS)], axis=1))

        routed = w[:, 0:1] * picked(0)
        for k in range(1, TOP_K):
            routed = routed + w[:, k:k + 1] * picked(k)
        x = x2_ref[...] + modf_ref[:, 2 * D_MODEL:3 * D_MODEL] * routed
        ms = jnp.mean(x * x, axis=-1, keepdims=True)
        o_ref[...] = x * lax.rsqrt(ms + RMS_EPS) * gfin_ref[...]

    @pl.when(jnp.logical_not(live))
    def _():
        o_ref[...] = jnp.zeros(o_ref.shape, F32)


def _final(x2, modf, wts, yg, g_final, tm, *, batch0=0, nbatch=None, wts_block0=0, yg_block0=0, prev=None,
           zero_rest=False):
    B, S, _ = x2.shape
    nt = S // tm
    nbatch = B - batch0 if nbatch is None else nbatch
    steps_b = B if zero_rest else nbatch
    bb = lambda b: jnp.minimum(b, nbatch - 1)
    ii = lambda b, i: jnp.where(b < nbatch, i, nt - 1)
    tok = lambda b, i: (batch0 + bb(b), ii(b, i), 0)
    mod_rows = modf.shape[1]
    mod_spec = (pl.BlockSpec((None, 1, 3 * D_MODEL), lambda b, i: (batch0 + bb(b), 0, 0)) if mod_rows == 1
                else pl.BlockSpec((None, tm, 3 * D_MODEL), tok))
    in_specs = [pl.BlockSpec((None, tm, D_MODEL), tok), mod_spec,
                pl.BlockSpec((tm, TOP_K), lambda b, i: (wts_block0 + bb(b) * nt + ii(b, i), 0)),
                pl.BlockSpec((TOP_K, ROW_CHUNKS, tm, CHUNK), lambda b, i: (0, 0, yg_block0 + bb(b) * nt + ii(b, i), 0)),
                _const_spec((1, D_MODEL))]
    operands = [x2, modf, wts, yg, g_final]
    aliases = {}
    if prev is not None:
        in_specs.append(pl.BlockSpec(memory_space=pl.ANY))
        operands.append(prev)
        aliases = {len(operands) - 1: 0}
    return pl.pallas_call(
        functools.partial(_final_kernel, live_batches=nbatch),
        grid=(steps_b, nt),
        in_specs=in_specs,
        out_specs=pl.BlockSpec((None, tm, D_MODEL), lambda b, i: (batch0 + b, i, 0)),
        out_shape=jax.ShapeDtypeStruct((B, S, D_MODEL), F32),
        input_output_aliases=aliases,
        compiler_params=_params(("arbitrary", "arbitrary")),
        name="final",
    )(*operands)


DEST_LANES = 1024


def _dest_kernel(pstart_ref, idx_ref, pos_ref, o_ref):
    idx = idx_ref[...]

    def add_start(e, acc):
        return acc + jnp.where(idx == e, pstart_ref[e], 0)

    o_ref[...] = lax.fori_loop(0, N_EXPERTS, add_start, pos_ref[...])


def _dest(pstart, idx_t, pos_t):
    n = idx_t.shape[1]
    spec = pl.BlockSpec((TOP_K, DEST_LANES), lambda i, ps: (0, i))
    return pl.pallas_call(
        _dest_kernel,
        grid_spec=pltpu.PrefetchScalarGridSpec(num_scalar_prefetch=1, grid=(pl.cdiv(n, DEST_LANES),),
                                               in_specs=[spec, spec], out_specs=spec),
        out_shape=jax.ShapeDtypeStruct((TOP_K, n), I32),
        compiler_params=_params(("arbitrary",)),
        name="dest",
    )(pstart, idx_t, pos_t)


def _sc_worker_id():
    return lax.axis_index("s") * 2 + lax.axis_index("c")


def _dispatch_rows(h_first, h_rest, dest_blocks, rows_out):
    nsteps = dest_blocks.shape[0]
    first_steps = h_first.shape[1] // SC_WINDOW
    assert first_steps + h_rest.shape[1] // SC_WINDOW == nsteps
    mesh = plsc.VectorSubcoreMesh(core_axis_name="c", subcore_axis_name="s")

    @functools.partial(
        pl.kernel, mesh=mesh,
        out_type=jax.ShapeDtypeStruct((ROW_CHUNKS, rows_out, CHUNK), F32),
        scratch_types=[pltpu.VMEM((TOP_K, SC_WINDOW), I32),
                       pltpu.VMEM((SC_WINDOW, CHUNK), F32),
                       pltpu.SemaphoreType.DMA],
    )
    def k(xa_hbm, xb_hbm, d_hbm, o_hbm, idx_v, rows_v, sem):
        def move(src_hbm, s, step_in_src):
            base = pl.multiple_of(step_in_src * SC_WINDOW, SC_WINDOW)
            pltpu.sync_copy(d_hbm.at[s], idx_v)
            for c in range(ROW_CHUNKS):
                pltpu.sync_copy(src_hbm.at[c, pl.ds(base, SC_WINDOW)], rows_v)
                scatters = [pltpu.async_copy(rows_v, o_hbm.at[c].at[idx_v.at[kk]], sem) for kk in range(TOP_K)]
                for cp in scatters:
                    cp.wait()

        @pl.loop(_sc_worker_id(), nsteps, step=SC_WORKERS)
        def _(s):
            @pl.when(s < first_steps)
            def _():
                move(xa_hbm, s, s)

            @pl.when(s >= first_steps)
            def _():
                move(xb_hbm, s, s - first_steps)

    return k(h_first, h_rest, dest_blocks)


def _gather_rows(y_sorted, dest_blocks):
    nsteps = dest_blocks.shape[0]
    ntok = nsteps * SC_WINDOW
    mesh = plsc.VectorSubcoreMesh(core_axis_name="c", subcore_axis_name="s")

    @functools.partial(
        pl.kernel, mesh=mesh,
        out_type=jax.ShapeDtypeStruct((TOP_K, ROW_CHUNKS, ntok, CHUNK), F32),
        scratch_types=[pltpu.VMEM((TOP_K, SC_WINDOW), I32),
                       pltpu.VMEM((SC_WINDOW, CHUNK), F32),
                       pltpu.VMEM((SC_WINDOW, CHUNK), F32),
                       pltpu.SemaphoreType.DMA((2,)),
                       pltpu.SemaphoreType.DMA((2,))],
    )
    def k(y_hbm, d_hbm, o_hbm, idx_v, buf0, buf1, sem_in, sem_out):
        bufs = (buf0, buf1)
        items = [(c, kk) for c in range(ROW_CHUNKS) for kk in range(TOP_K)]

        @pl.loop(_sc_worker_id(), nsteps, step=SC_WORKERS)
        def _(s):
            base = pl.multiple_of(s * SC_WINDOW, SC_WINDOW)
            pltpu.sync_copy(d_hbm.at[s], idx_v)

            def fetch(i):
                c, kk = items[i]
                return pltpu.async_copy(y_hbm.at[c].at[idx_v.at[kk]], bufs[i % 2], sem_in.at[i % 2])

            def store(i):
                c, kk = items[i]
                return pltpu.async_copy(bufs[i % 2], o_hbm.at[kk, c, pl.ds(base, SC_WINDOW)], sem_out.at[i % 2])

            pending_store = [None, None]
            fetched = fetch(0)
            for i in range(len(items)):
                nxt = None
                if i + 1 < len(items):
                    if pending_store[(i + 1) % 2] is not None:
                        pending_store[(i + 1) % 2].wait()
                    nxt = fetch(i + 1)
                fetched.wait()
                pending_store[i % 2] = store(i)
                fetched = nxt
            pending_store[0].wait()
            pending_store[1].wait()

    return k(y_sorted, dest_blocks)


PAST_LEN = 8192


def _rope_tables(pos):
    half = HEAD_DIM // 2
    inv_freq = ROPE_THETA ** (-jnp.arange(half, dtype=F32) / half)
    ang = pos.astype(F32)[:, None] * inv_freq[None, :]
    cos, sin = jnp.cos(ang), jnp.sin(ang)
    reps = LANES // HEAD_DIM
    return jnp.tile(cos, (1, 2 * reps)), jnp.tile(jnp.concatenate([-sin, sin], axis=1), (1, reps))


def _rope_parts(seq, tm):
    half = HEAD_DIM // 2
    inv_freq = ROPE_THETA ** (-jnp.arange(half, dtype=F32) / half)
    inv_l = jnp.tile(inv_freq, LANES // half)
    start = (jnp.arange(seq // tm, dtype=I32) * tm).astype(F32)
    ang_b = start[:, None] * inv_l[None, :]
    parts = [jnp.stack([jnp.cos(ang_b), jnp.sin(ang_b)], axis=1)]
    r = jnp.arange(tm, dtype=I32)
    for d in DILATIONS:
        n = tm // d
        offset = ((r % n) * d + r // n).astype(F32)
        ang = offset[:, None] * inv_l[None, :]
        parts += [jnp.cos(ang), jnp.sin(ang)]
    return parts


def _kv_tail(kg, vg, keep):
    b, d, l, _ = kg.shape
    n = keep // d

    def natural(t):
        t = t[:, :, l - n:, :].transpose(0, 2, 1, 3)
        return t.reshape(b, keep, HEADS, HEAD_DIM)

    return jnp.stack([natural(kg), natural(vg)], axis=2).astype(F32)[None]


def kernel(x_prompt, x_sample, cache_kv_w128, cache_kv_w512, cache_kv_w2048, state_conv, c_prompt, c_sample,
           g_mix, w_ada_mix, b_ada_mix, w_in, w_dw, b_dw, ln_conv_g, ln_conv_b, w_conv_out, w_att_out, w_o,
           g_ffn, w_ada_ffn, b_ada_ffn, w_router, b_router, w_exp_gate, w_exp_up, w_exp_down,
           w_sh_gate, w_sh_up, w_sh_down, g_final):
    B, S, _ = x_prompt.shape
    ns, T, _ = x_sample.shape
    assert g_mix.shape[0] == 1 and T == 1
    tm = min(TOKEN_TILE, S)
    span = DILATIONS[2] * 16
    assert S % tm == 0 and tm % span == 0 and S % (DILATIONS[2] * Q_BLOCK) == 0
    assert ns % SC_WINDOW == 0 and (B * S) % ns == 0
    ntok = B * S + ns
    caches = (cache_kv_w128, cache_kv_w512, cache_kv_w2048)

    row = lambda v: v.reshape(1, -1)
    w_in_b = w_in[0].astype(BF16)
    w_co_b = w_conv_out[0].astype(BF16)
    w_ao_b = w_att_out[0].astype(BF16)
    w_o_b = w_o[0].astype(BF16)
    wsg_b = w_sh_gate[0].astype(BF16)
    wsu_b = w_sh_up[0].astype(BF16)
    wsd_b = w_sh_down[0].astype(BF16)

    n_c = B + ns
    c_all = jnp.concatenate([c_prompt, c_sample], axis=0)
    c_all = jnp.pad(c_all, ((0, -n_c % 8), (0, 0)))
    mod_mix = _ada(c_all, w_ada_mix[0], b_ada_mix[0])
    mod_ffn = _ada(c_all, w_ada_ffn[0], b_ada_ffn[0])
    modm_p = mod_mix[:B].reshape(B, 1, 3 * D_MODEL)
    modf_p = mod_ffn[:B].reshape(B, 1, 3 * D_MODEL)
    modm_s = mod_mix[B:n_c].reshape(1, ns, 3 * D_MODEL)
    modf_s = mod_ffn[B:n_c].reshape(1, ns, 3 * D_MODEL)

    tables = _rope_parts(S, tm)
    (q0, k0, v0, q1, k1, v1, q2, k2, v2, gc_p, ga_p, utail) = _inproj_prompt(
        x_prompt, modm_p, row(g_mix[0]), w_in_b, tables, w_dw[0], row(b_dw[0]),
        row(ln_conv_g[0]), row(ln_conv_b[0]), w_co_b, tm)
    conv_prompt = utail[:, CONV_HALO - (CONV_WIDTH - 1):][None]

    attn_in = []
    lse_in = []
    kv_prompt = []
    for (qg, kg, vg), d in zip(((q0, k0, v0), (q1, k1, v1), (q2, k2, v2)), DILATIONS):
        l = S // d
        flat = lambda t: t.reshape(B * d, l, D_SLOT)
        o, lse = _attn_prompt(flat(qg), flat(kg), flat(vg))
        if d == 1:
            attn_in.append(o.reshape(B, S, D_SLOT))
            lse_in.append(lse.reshape(B, S, LANES))
        else:
            attn_in.append(o.reshape(B, d, l, D_SLOT))
            lse_in.append(lse.reshape(B, d, l, LANES))
        keep = min(Q_BLOCK * d, S)
        kv_prompt.append(_kv_tail(kg.reshape(B, d, l, D_SLOT), vg.reshape(B, d, l, D_SLOT), keep))

    wr_t = w_router[0].T
    wr_hi = wr_t.astype(BF16)
    wr_lo = (wr_t - wr_hi.astype(F32)).astype(BF16)
    b_col = b_router[0].reshape(N_EXPERTS, 1)
    zero_carry = jnp.zeros((N_EXPERTS, 1), F32)
    h2_p, x2_p, idx_p, wts_p, pos_p, cnt_p = _post(
        x_prompt, modm_p, (*attn_in, *lse_in), gc_p, ga_p, w_ao_b, w_o_b, row(g_ffn[0]), modf_p,
        wr_hi, wr_lo, b_col, wsg_b, wsu_b, wsd_b, zero_carry, tm, True)

    cos_s, sin_s = _rope_tables(jnp.full((1,), PAST_LEN, I32))
    u_s, q_s, k_s, v_s, sgc_s, ga_s = _inproj_sample(
        x_sample.reshape(ns, D_MODEL), mod_mix[B:n_c], row(g_mix[0]), w_in_b, cos_s, sin_s)
    heads = lambda t: t.reshape(ns, N_GROUPS, 1, HEADS, HEAD_DIM)
    qkv_rows = jnp.concatenate([heads(q_s), heads(k_s), heads(v_s)], axis=2)
    kv_sample, outs_s, lses_s = [], [], []
    for gi, (cache, d) in enumerate(zip(caches, DILATIONS)):
        cache_t = jnp.transpose(cache[0], (0, 2, 3, 4, 1))
        new_t, o, lse = _sample_cache(cache_t, qkv_rows, gi, d)
        kv_sample.append(jnp.transpose(new_t, (0, 4, 1, 2, 3))[None])
        outs_s.append(o.reshape(ns, D_SLOT))
        lses_s.append(lse.reshape(ns, D_SLOT))
    state_t = jnp.transpose(state_conv[0], (1, 0, 2))
    gc_s, att_s, cst_t = _sample_mix(state_t, u_s, sgc_s, w_dw[0], row(b_dw[0]), row(ln_conv_g[0]),
                                     row(ln_conv_b[0]), w_co_b, outs_s, lses_s)
    conv_sample = jnp.transpose(cst_t, (1, 0, 2))[None]
    as3 = lambda t: t.reshape(1, ns, t.shape[-1])
    h2_s, x2_s, idx_s, wts_s, pos_s, cnt = _post(
        as3(x_sample.reshape(ns, D_MODEL)), modm_s, (as3(att_s),), as3(gc_s), as3(ga_s), w_ao_b, w_o_b,
        row(g_ffn[0]), modf_s, wr_hi, wr_lo, b_col, wsg_b, wsu_b, wsd_b, cnt_p, ns, False)

    bm = EXPERT_BLOCK
    nblk = (ntok * TOP_K + N_EXPERTS * (bm - 1)) // bm
    counts = cnt[:, 0].astype(I32)
    padded = (counts + bm - 1) // bm * bm
    pend = jnp.cumsum(padded)
    pstart = pend - padded
    idx_all = jnp.concatenate([idx_p, idx_s], axis=1)
    pos_all = jnp.concatenate([pos_p, pos_s], axis=1)
    dest = _dest(pstart, idx_all, pos_all)
    dest_blocks = dest.reshape(TOP_K, ntok // SC_WINDOW, SC_WINDOW).transpose(1, 0, 2)
    wts_tok = jnp.concatenate([wts_p, wts_s], axis=1).T
    blk_row0 = jnp.arange(nblk, dtype=I32) * bm
    blk_e = jnp.minimum(jnp.sum((pend[None, :] <= blk_row0[:, None]).astype(I32), axis=1), N_EXPERTS - 1)
    mine = blk_e[:, None] == jnp.arange(N_EXPERTS, dtype=I32)[None, :]
    cnt_b = jnp.sum(jnp.where(mine, counts[None, :], 0), axis=1)
    start_b = jnp.sum(jnp.where(mine, pstart[None, :], 0), axis=1)
    blk_valid = jnp.clip(cnt_b - (blk_row0 - start_b), 0, bm).astype(I32)
    n_used = (pend[N_EXPERTS - 1] // bm).reshape(1)

    x_sorted = _dispatch_rows(h2_p, h2_s, dest_blocks, nblk * bm)
    y_sorted = _experts(blk_e, blk_valid, n_used, x_sorted, w_exp_gate[0], w_exp_up[0], w_exp_down[0], bm)
    b_split = B // 2
    g_fin = row(g_final)
    if b_split > 0:
        steps_a = b_split * S // SC_WINDOW
        yg_a = _gather_rows(y_sorted, dest_blocks[:steps_a])
        yg_a, dest_b = lax.optimization_barrier((yg_a, dest_blocks[steps_a:]))
        yg_b = _gather_rows(y_sorted, dest_b)
        y_first = _final(x2_p, modf_p, wts_tok, yg_a, g_fin, tm, nbatch=b_split, zero_rest=True)
        y_prompt = _final(x2_p, modf_p, wts_tok, yg_b, g_fin, tm, batch0=b_split,
                          wts_block0=b_split * (S // tm), prev=y_first)
        yg_s, yg_s_block0 = yg_b, (B - b_split) * S // ns
    else:
        yg_s = _gather_rows(y_sorted, dest_blocks)
        y_prompt = _final(x2_p, modf_p, wts_tok, yg_s, g_fin, tm)
        yg_s_block0 = B * S // ns
    y_sample = _final(x2_s, modf_s, wts_tok, yg_s, g_fin, ns, wts_block0=B * S // ns,
                      yg_block0=yg_s_block0).reshape(ns, 1, D_MODEL)

    return (y_prompt, y_sample, kv_prompt[0], kv_prompt[1], kv_prompt[2], conv_prompt,
            kv_sample[0], kv_sample[1], kv_sample[2], conv_sample)
```
